```python
import math
import jax
import jax.numpy as jnp
from jax import lax
import numpy as np


D_MODEL = 2048
BATCH = 2
SEQ = 4096
DEPTH = 2

HEAD_DIM = 64
GRID_W = 64
NA_HEADS = 16
NA_ROWS = 8
NA_COLS = 16
DIFF_HEADS = 8
DIL_HEADS = 32
DIL_CONFIGS = ((128, 1), (512, 4), (2048, 16))
N_DIL = len(DIL_CONFIGS)
ROPE_THETA = 10000.0
Q_BLOCK = 128
N_EXPERTS = 32
TOP_K = 4
D_FF = 2048
SWIGLU_LIMIT = 7.0
SWIGLU_ALPHA = 1.702
MOE_BLOCK = 256
RMS_EPS = 1e-5
NA_WIDTH = NA_HEADS * HEAD_DIM
DIFF_WIDTH = DIFF_HEADS * 2 * HEAD_DIM
AB_IN = 3 * NA_WIDTH + 3 * DIFF_WIDTH
AB_OUT = NA_WIDTH + DIFF_WIDTH
DIL_WIDTH = DIL_HEADS * HEAD_DIM
DIL_IN = 3 * N_DIL * DIL_WIDTH
N_EVEN = (DEPTH + 1) // 2
N_ODD = DEPTH // 2

kernel_name = 'hybrid_natten_diff_dilated_moe_encoder'


def rms_norm(x, g):
    xf = x.astype(jnp.float32)
    y = xf * lax.rsqrt(jnp.mean(xf * xf, axis=-1, keepdims=True) + RMS_EPS)
    return (y * g.astype(jnp.float32)).astype(x.dtype)


def adaln(c_act, w, b):
    mod = c_act @ w + b
    shift, scale, gate = jnp.split(mod, 3, axis=-1)
    return shift[:, None, :], scale[:, None, :], gate[:, None, :]


def rope_tables(seq):
    inv_freq = ROPE_THETA ** (-jnp.arange(0, HEAD_DIM, 2, dtype=jnp.float32) / HEAD_DIM)
    ang = jnp.arange(seq, dtype=jnp.float32)[:, None] * inv_freq[None, :]
    return jnp.cos(ang), jnp.sin(ang)


def apply_rope(x, cos, sin):
    x1, x2 = jnp.split(x, 2, axis=-1)
    cos = cos.astype(x.dtype)
    sin = sin.astype(x.dtype)
    return jnp.concatenate([x1 * cos - x2 * sin, x2 * cos + x1 * sin], axis=-1)


def neighbourhood_attention(q, k, v, rpb):
    b, nh, s, dh = q.shape
    rows = s // GRID_W
    kr = min(NA_ROWS, rows)
    kc = NA_COLS
    qg = q.reshape(b, nh, rows, GRID_W, dh) * (dh ** -0.5)
    kg = k.reshape(b, nh, rows, GRID_W, dh)
    vg = v.reshape(b, nh, rows, GRID_W, dh)
    r = jnp.arange(rows)
    col = jnp.arange(GRID_W)
    row_start = jnp.clip(r - kr // 2, 0, rows - kr)
    col_idx = jnp.clip(col - kc // 2, 0, GRID_W - kc)[:, None] + jnp.arange(kc)[None, :]
    col_off = col_idx - col[:, None] + (NA_COLS - 1)
    rpb_cols = rpb[:, :, col_off]

    def row_block(args):
        q_row, rs, r_i = args
        k_nb = lax.dynamic_slice_in_dim(kg, rs, kr, axis=2)[:, :, :, col_idx]
        v_nb = lax.dynamic_slice_in_dim(vg, rs, kr, axis=2)[:, :, :, col_idx]
        row_off = rs + jnp.arange(kr) - r_i + (NA_ROWS - 1)
        bias = jnp.transpose(rpb_cols[:, row_off], (0, 2, 1, 3))
        sc = jnp.einsum('bhwd,bhiwjd->bhwij', q_row, k_nb).astype(jnp.float32) + bias[None].astype(jnp.float32)
        p = jax.nn.softmax(sc.reshape(b, nh, GRID_W, kr * kc), axis=-1).reshape(sc.shape)
        return jnp.einsum('bhwij,bhiwjd->bhwd', p.astype(v.dtype), v_nb)

    out = lax.map(row_block, (jnp.moveaxis(qg, 2, 0), row_start, r))
    return jnp.moveaxis(out, 0, 2).reshape(b, nh, s, dh)


def differential_attention(q, k, v, lam_q1, lam_k1, lam_q2, lam_k2, subln_g, lambda_init):
    b, nh, _, s, dh = q.shape
    nb = s // Q_BLOCK
    f32 = jnp.float32
    lam = (jnp.exp(jnp.sum(lam_q1.astype(f32) * lam_k1.astype(f32)))
           - jnp.exp(jnp.sum(lam_q2.astype(f32) * lam_k2.astype(f32))) + lambda_init)
    q_blocks = jnp.moveaxis((q * (dh ** -0.5)).reshape(b, nh, 2, nb, Q_BLOCK, dh), 3, 0)

    def block(q_blk):
        sc = jnp.einsum('bhmqd,bhmkd->bhmqk', q_blk, k).astype(f32)
        p = jax.nn.softmax(sc, axis=-1)
        a = p[:, :, 0] - lam * p[:, :, 1]
        return jnp.einsum('bhqk,bhkd->bhqd', a.astype(v.dtype), v)

    out = lax.map(block, q_blocks)
    out = jnp.moveaxis(out, 0, 2).reshape(b, nh, s, 2 * dh)
    return rms_norm(out, subln_g) * (1.0 - lambda_init)


def dilated_attention(q, k, v):
    g_n, b, nh, s, dh = q.shape
    nb = s // Q_BLOCK
    q_blocks = jnp.moveaxis((q * (dh ** -0.5)).reshape(g_n, b, nh, nb, Q_BLOCK, dh), 3, 0)

    def block(args):
        q_blk, blk = args
        pos = blk * Q_BLOCK + jnp.arange(Q_BLOCK)
        outs = []
        lses = []
        for gi, (window, dil) in enumerate(DIL_CONFIGS):
            n_side = (window // 2) // dil
            idx = pos[:, None] + dil * jnp.arange(-n_side, n_side + 1)[None, :]
            valid = (idx >= 0) & (idx < s)
            idx = jnp.clip(idx, 0, s - 1)
            k_g = k[gi][:, :, idx]
            v_g = v[gi][:, :, idx]
            sc = jnp.einsum('bhqd,bhqkd->bhqk', q_blk[gi], k_g).astype(jnp.float32)
            sc = jnp.where(valid[None, None], sc, -jnp.inf)
            m = jnp.max(sc, axis=-1, keepdims=True)
            e = jnp.exp(sc - m)
            den = jnp.sum(e, axis=-1, keepdims=True)
            outs.append(jnp.einsum('bhqk,bhqkd->bhqd', (e / den).astype(v.dtype), v_g))
            lses.append(m[..., 0] + jnp.log(den[..., 0]))
        w_mix = jax.nn.softmax(jnp.stack(lses, axis=0), axis=0)
        return jnp.sum(w_mix[..., None].astype(v.dtype) * jnp.stack(outs, axis=0), axis=0)

    out = lax.map(block, (q_blocks, jnp.arange(nb)))
    return jnp.moveaxis(out, 0, 2).reshape(b, nh, s, dh)


def mixer_na_diff(h, w_in, w_out, rpb, lq1, lk1, lq2, lk2, subln_g, cos, sin, lambda_init):
    b, s, _ = h.shape
    proj = h @ w_in
    na_qkv, d_q, d_k, d_v = jnp.split(proj, [3 * NA_WIDTH, 3 * NA_WIDTH + DIFF_WIDTH, 3 * NA_WIDTH + 2 * DIFF_WIDTH], axis=-1)
    na_qkv = jnp.transpose(na_qkv.reshape(b, s, 3, NA_HEADS, HEAD_DIM), (2, 0, 3, 1, 4))
    o_na = neighbourhood_attention(na_qkv[0], na_qkv[1], na_qkv[2], rpb)
    dq = apply_rope(jnp.transpose(d_q.reshape(b, s, DIFF_HEADS, 2, HEAD_DIM), (0, 2, 3, 1, 4)), cos, sin)
    dk = apply_rope(jnp.transpose(d_k.reshape(b, s, DIFF_HEADS, 2, HEAD_DIM), (0, 2, 3, 1, 4)), cos, sin)
    dv = jnp.transpose(d_v.reshape(b, s, DIFF_HEADS, 2 * HEAD_DIM), (0, 2, 1, 3))
    o_df = differential_attention(dq, dk, dv, lq1, lk1, lq2, lk2, subln_g, lambda_init)
    o = jnp.concatenate([jnp.transpose(o_na, (0, 2, 1, 3)).reshape(b, s, NA_WIDTH),
                         jnp.transpose(o_df, (0, 2, 1, 3)).reshape(b, s, DIFF_WIDTH)], axis=-1)
    return o @ w_out


def mixer_dilated(h, w_in, w_out, cos, sin):
    b, s, _ = h.shape
    qkv = jnp.transpose((h @ w_in).reshape(b, s, N_DIL, 3, DIL_HEADS, HEAD_DIM), (3, 2, 0, 4, 1, 5))
    q = apply_rope(qkv[0], cos, sin)
    k = apply_rope(qkv[1], cos, sin)
    o = dilated_attention(q, k, qkv[2])
    return jnp.transpose(o, (0, 2, 1, 3)).reshape(b, s, DIL_WIDTH) @ w_out


def moe_ffn(h, router_w, router_b, w_gu, b_gu, w_dn, b_dn):
    b, s, d = h.shape
    t = b * s
    xt = h.reshape(t, d)
    logits = (xt @ router_w + router_b).astype(jnp.float32)
    top_vals, top_idx = lax.top_k(logits, TOP_K)
    gates = jax.nn.softmax(top_vals, axis=-1)
    n_assign = t * TOP_K
    flat_e = top_idx.reshape(-1)
    flat_tok = jnp.repeat(jnp.arange(t, dtype=jnp.int32), TOP_K)
    flat_g = gates.reshape(-1)
    order = jnp.argsort(flat_e)
    se = flat_e[order]
    stok = flat_tok[order]
    sg = flat_g[order]
    counts = jnp.bincount(flat_e, length=N_EXPERTS)
    starts = jnp.cumsum(counts) - counts
    padded = (counts + MOE_BLOCK - 1) // MOE_BLOCK * MOE_BLOCK
    pends = jnp.cumsum(padded)
    pstarts = pends - padded
    n_slots = n_assign + N_EXPERTS * MOE_BLOCK
    n_blocks = n_slots // MOE_BLOCK
    dest = pstarts[se] + jnp.arange(n_assign) - starts[se]
    slot_tok = jnp.zeros((n_slots,), jnp.int32).at[dest].set(stok)
    slot_g = jnp.zeros((n_slots,), jnp.float32).at[dest].set(sg)
    block_e = jnp.clip(jnp.searchsorted(pends, jnp.arange(n_blocks) * MOE_BLOCK, side='right'), 0, N_EXPERTS - 1)

    def block(args):
        e, toks, g = args
        xb = xt[toks]
        hgu = xb @ w_gu[e] + b_gu[e]
        glu = jnp.minimum(hgu[:, ::2], SWIGLU_LIMIT)
        lin = jnp.clip(hgu[:, 1::2], -SWIGLU_LIMIT, SWIGLU_LIMIT)
        act = glu * jax.nn.sigmoid(SWIGLU_ALPHA * glu) * (lin + 1.0)
        out = act @ w_dn[e] + b_dn[e]
        return out * g[:, None].astype(out.dtype)

    y_slots = lax.map(block, (block_e, slot_tok.reshape(n_blocks, MOE_BLOCK), slot_g.reshape(n_blocks, MOE_BLOCK)))
    y = jnp.zeros((t, d), h.dtype).at[slot_tok].add(y_slots.reshape(n_slots, d))
    return y.reshape(b, s, d)


def setup_inputs(seed: int = 0) -> dict:
    key = jax.random.key(seed)
    ks = jax.random.split(key, 22)
    d = D_MODEL

    def nrm(k, shape, sc):
        return jax.random.normal(k, shape, jnp.float32) * sc

    return {
        'x': nrm(ks[0], (BATCH, SEQ, d), 1.0),
        'c': nrm(ks[1], (BATCH, d), 1.0),
        'ada_w': nrm(ks[2], (DEPTH, 2, d, 3 * d), 0.5 * d ** -0.5),
        'ada_b': nrm(ks[3], (DEPTH, 2, 3 * d), 0.02),
        'norm_g': 1.0 + nrm(ks[4], (DEPTH, 2, d), 0.02),
        'final_g': 1.0 + nrm(ks[5], (d,), 0.02),
        'ab_w_in': nrm(ks[6], (N_EVEN, d, AB_IN), d ** -0.5),
        'ab_w_out': nrm(ks[7], (N_EVEN, AB_OUT, d), AB_OUT ** -0.5),
        'na_rpb': nrm(ks[8], (N_EVEN, NA_HEADS, 2 * NA_ROWS - 1, 2 * NA_COLS - 1), 0.1),
        'diff_lam_q1': nrm(ks[9], (N_EVEN, HEAD_DIM), 0.1),
        'diff_lam_k1': nrm(ks[10], (N_EVEN, HEAD_DIM), 0.1),
        'diff_lam_q2': nrm(ks[11], (N_EVEN, HEAD_DIM), 0.1),
        'diff_lam_k2': nrm(ks[12], (N_EVEN, HEAD_DIM), 0.1),
        'diff_subln_g': 1.0 + nrm(ks[13], (N_EVEN, 2 * HEAD_DIM), 0.02),
        'dil_w_in': nrm(ks[14], (N_ODD, d, DIL_IN), d ** -0.5),
        'dil_w_out': nrm(ks[15], (N_ODD, DIL_WIDTH, d), DIL_WIDTH ** -0.5),
        'moe_router_w': nrm(ks[16], (DEPTH, d, N_EXPERTS), d ** -0.5),
        'moe_router_b': nrm(ks[17], (DEPTH, N_EXPERTS), 0.01),
        'moe_w_gate_up': nrm(ks[18], (DEPTH, N_EXPERTS, d, 2 * D_FF), d ** -0.5),
        'moe_b_gate_up': nrm(ks[19], (DEPTH, N_EXPERTS, 2 * D_FF), 0.02),
        'moe_w_down': nrm(ks[20], (DEPTH, N_EXPERTS, D_FF, d), D_FF ** -0.5),
        'moe_b_down': nrm(ks[21], (DEPTH, N_EXPERTS, d), 0.02),
    }


def reference(x, c, ada_w, ada_b, norm_g, final_g, ab_w_in, ab_w_out, na_rpb, diff_lam_q1, diff_lam_k1,
              diff_lam_q2, diff_lam_k2, diff_subln_g, dil_w_in, dil_w_out, moe_router_w, moe_router_b,
              moe_w_gate_up, moe_b_gate_up, moe_w_down, moe_b_down):
    cos, sin = rope_tables(x.shape[1])
    c_act = jax.nn.silu(c)
    for l in range(DEPTH):
        shift, scale, gate = adaln(c_act, ada_w[l, 0], ada_b[l, 0])
        h = rms_norm(x, norm_g[l, 0]) * (1.0 + scale) + shift
        i = l // 2
        if l % 2 == 0:
            lambda_init = 0.8 - 0.6 * math.exp(-0.3 * l)
            mix = mixer_na_diff(h, ab_w_in[i], ab_w_out[i], na_rpb[i], diff_lam_q1[i], diff_lam_k1[i],
                                diff_lam_q2[i], diff_lam_k2[i], diff_subln_g[i], cos, sin, lambda_init)
        else:
            mix = mixer_dilated(h, dil_w_in[i], dil_w_out[i], cos, sin)
        x = x + gate * mix
        shift, scale, gate = adaln(c_act, ada_w[l, 1], ada_b[l, 1])
        h = rms_norm(x, norm_g[l, 1]) * (1.0 + scale) + shift
        x = x + gate * moe_ffn(h, moe_router_w[l], moe_router_b[l], moe_w_gate_up[l], moe_b_gate_up[l],
                               moe_w_down[l], moe_b_down[l])
    return rms_norm(x, final_g)
```

```python
import functools
import math

import jax
import jax.numpy as jnp
from jax import lax
from jax.experimental import pallas as pl
from jax.experimental.pallas import tpu as pltpu

F32 = jnp.float32
BF16 = jnp.bfloat16

HEAD_DIM = 64
LANES = 128
GRID_W = 64
NA_HEADS = 16
NA_ROWS = 8
NA_COLS = 16
DIFF_HEADS = 8
DIL_HEADS = 32
DIL_CONFIGS = ((128, 1), (512, 4), (2048, 16))
ROPE_THETA = 10000.0
N_EXPERTS = 32
TOP_K = 4
SWIGLU_LIMIT = 7.0
SWIGLU_ALPHA = 1.702
RMS_EPS = 1e-5

DIL_QBLOCK = 128
MOE_SUB = 256
MOE_ITEM_ROWS = 1536
MOE_FT = 128
ROW_SLAB = 16
VMEM_LIMIT = 56 * 1024 * 1024


def _cparams(sem, vmem=VMEM_LIMIT):
    return pltpu.CompilerParams(dimension_semantics=sem, vmem_limit_bytes=vmem)


def _dot(a, b):
    return jnp.dot(a, b, preferred_element_type=F32)


def _dot_nt(a, b):
    return lax.dot_general(a, b, (((1,), (1,)), ((), ())), preferred_element_type=F32)


def _split_bf16(a):
    hi = a.astype(BF16)
    lo = (a - hi.astype(F32)).astype(BF16)
    return hi, lo


def _dot3(a, b):
    ah, al = _split_bf16(a)
    bh, bl = _split_bf16(b)
    return _dot(ah, bh) + (_dot(ah, bl) + _dot(al, bh))


def _adaln_kernel(c_ref, w_ref, b_ref, o_ref):
    c = c_ref[...]
    c_act = c * jax.nn.sigmoid(c)
    o_ref[0] = _dot3(c_act, w_ref[0]) + b_ref[0]


def _adaln(c_pad, w, b, tn=768):
    n_mod, d, n = w.shape
    return pl.pallas_call(
        _adaln_kernel,
        grid=(n_mod, n // tn),
        in_specs=[
            pl.BlockSpec(c_pad.shape, lambda l, j: (0, 0)),
            pl.BlockSpec((1, d, tn), lambda l, j: (l, 0, j)),
            pl.BlockSpec((1, 1, tn), lambda l, j: (l, 0, j)),
        ],
        out_specs=pl.BlockSpec((1, c_pad.shape[0], tn), lambda l, j: (l, 0, j)),
        out_shape=jax.ShapeDtypeStruct((n_mod, c_pad.shape[0], n), F32),
        compiler_params=_cparams(("arbitrary", "arbitrary")),
        name="adaln",
    )(c_pad, w, b.reshape(n_mod, 1, n))


def _rms(x, g):
    return x * lax.rsqrt(jnp.mean(x * x, axis=-1, keepdims=True) + RMS_EPS) * g


def _norm_mod_kernel(x_ref, g_ref, scale_ref, shift_ref, o_ref):
    h = _rms(x_ref[0], g_ref[...]) * (1.0 + scale_ref[0]) + shift_ref[0]
    o_ref[0] = h.astype(o_ref.dtype)


def _norm_mod(x, g, scale, shift, out_dtype, tm=512):
    b, s, d = x.shape
    return pl.pallas_call(
        _norm_mod_kernel,
        grid=(b, s // tm),
        in_specs=[
            pl.BlockSpec((1, tm, d), lambda bi, i: (bi, i, 0)),
            pl.BlockSpec((1, d), lambda bi, i: (0, 0)),
            pl.BlockSpec((1, 1, d), lambda bi, i: (bi, 0, 0)),
            pl.BlockSpec((1, 1, d), lambda bi, i: (bi, 0, 0)),
        ],
        out_specs=pl.BlockSpec((1, tm, d), lambda bi, i: (bi, i, 0)),
        out_shape=jax.ShapeDtypeStruct((b, s, d), out_dtype),
        compiler_params=_cparams(("arbitrary", "arbitrary")),
        name="norm_mod",
    )(x, g.reshape(1, d), scale, shift)


def _final_norm_kernel(x_ref, g_ref, o_ref):
    o_ref[0] = _rms(x_ref[0], g_ref[...])


def _final_norm(x, g, tm=512):
    b, s, d = x.shape
    return pl.pallas_call(
        _final_norm_kernel,
        grid=(b, s // tm),
        in_specs=[
            pl.BlockSpec((1, tm, d), lambda bi, i: (bi, i, 0)),
            pl.BlockSpec((1, d), lambda bi, i: (0, 0)),
        ],
        out_specs=pl.BlockSpec((1, tm, d), lambda bi, i: (bi, i, 0)),
        out_shape=jax.ShapeDtypeStruct((b, s, d), F32),
        compiler_params=_cparams(("arbitrary", "arbitrary")),
        name="final_norm",
    )(x, g.reshape(1, d))


def _proj_kernel(mode_ref, a_ref, w_ref, cos_ref, sin_ref, o_ref, wbf_ref):
    j = pl.program_id(0)

    @pl.when((pl.program_id(1) == 0) & (pl.program_id(2) == 0))
    def _():
        wbf_ref[...] = w_ref[...].astype(BF16)

    acc = _dot(a_ref[0], wbf_ref[...])
    mode = mode_ref[j]
    tn = acc.shape[1]

    @pl.when(mode == 0)
    def _():
        o_ref[0] = acc.astype(o_ref.dtype)

    @pl.when(mode != 0)
    def _():
        rope_on = (mode & 1).astype(F32)
        qscale = jnp.where((mode & 2) == 2, HEAD_DIM ** -0.5, 1.0).astype(F32)
        lane = lax.broadcasted_iota(jnp.int32, (1, LANES), 1)
        first_half = (lane % HEAD_DIM) < (HEAD_DIM // 2)
        cos = (cos_ref[...] * rope_on + (1.0 - rope_on)) * qscale
        sin = sin_ref[...] * (rope_on * qscale)
        for ci in range(tn // LANES):
            x = acc[:, ci * LANES:(ci + 1) * LANES]
            partner = jnp.where(first_half,
                                pltpu.roll(x, LANES - HEAD_DIM // 2, 1),
                                pltpu.roll(x, HEAD_DIM // 2, 1))
            o_ref[0, :, ci * LANES:(ci + 1) * LANES] = (x * cos + partner * sin).astype(o_ref.dtype)


def _proj(a, w, modes, cos_t, sin_t, tm=1024, tn=512):
    b, s, k = a.shape
    n = w.shape[1]
    grid_spec = pltpu.PrefetchScalarGridSpec(
        num_scalar_prefetch=1,
        grid=(n // tn, b, s // tm),
        in_specs=[
            pl.BlockSpec((1, tm, k), lambda j, bi, i, m: (bi, i, 0)),
            pl.BlockSpec((k, tn), lambda j, bi, i, m: (0, j)),
            pl.BlockSpec((tm, LANES), lambda j, bi, i, m: (i, 0)),
            pl.BlockSpec((tm, LANES), lambda j, bi, i, m: (i, 0)),
        ],
        out_specs=pl.BlockSpec((1, tm, tn), lambda j, bi, i, m: (bi, i, j)),
        scratch_shapes=[pltpu.VMEM((k, tn), BF16)],
    )
    return pl.pallas_call(
        _proj_kernel,
        grid_spec=grid_spec,
        out_shape=jax.ShapeDtypeStruct((b, s, n), BF16),
        compiler_params=_cparams(("arbitrary", "arbitrary", "arbitrary")),
        name="proj",
    )(modes, a, w, cos_t, sin_t)


def _outproj_kernel(a_ref, w_ref, x_ref, gate_ref, o_ref, wbf_ref):
    @pl.when((pl.program_id(1) == 0) & (pl.program_id(2) == 0))
    def _():
        wbf_ref[...] = w_ref[...].astype(BF16)

    o_ref[0] = x_ref[0] + gate_ref[0] * _dot(a_ref[0], wbf_ref[...])


def _outproj(a, w, x, gate, tm=1024, tn=512):
    b, s, k = a.shape
    n = w.shape[1]
    return pl.pallas_call(
        _outproj_kernel,
        grid=(n // tn, b, s // tm),
        in_specs=[
            pl.BlockSpec((1, tm, k), lambda j, bi, i: (bi, i, 0)),
            pl.BlockSpec((k, tn), lambda j, bi, i: (0, j)),
            pl.BlockSpec((1, tm, tn), lambda j, bi, i: (bi, i, j)),
            pl.BlockSpec((1, 1, tn), lambda j, bi, i: (bi, 0, j)),
        ],
        out_specs=pl.BlockSpec((1, tm, tn), lambda j, bi, i: (bi, i, j)),
        out_shape=jax.ShapeDtypeStruct((b, s, n), F32),
        scratch_shapes=[pltpu.VMEM((k, tn), BF16)],
        compiler_params=_cparams(("arbitrary", "arbitrary", "arbitrary")),
        name="outproj",
    )(a, w, x, gate)


def _columns(cols):
    m = cols[0].shape[0]
    lane = lax.broadcasted_iota(jnp.int32, (m, len(cols)), 1)
    out = jnp.broadcast_to(cols[0], (m, len(cols)))
    for k in range(1, len(cols)):
        out = jnp.where(lane == k, cols[k], out)
    return out


def _head_masks():
    lane = lax.broadcasted_iota(jnp.int32, (1, LANES), 1)
    return lane < HEAD_DIM, lane >= HEAD_DIM


def _na_kernel(q_ref, k_ref, v_ref, bias_ref, o_ref, *, rows):
    kr = min(NA_ROWS, rows)
    masks = _head_masks()

    def body(r, carry):
        rs = jnp.clip(r - kr // 2, 0, rows - kr)
        var = r - rs
        q = q_ref[0, pl.ds(pl.multiple_of(r * GRID_W, GRID_W), GRID_W), :]
        k = k_ref[0, pl.ds(pl.multiple_of(rs * GRID_W, GRID_W), kr * GRID_W), :]
        v = v_ref[0, pl.ds(pl.multiple_of(rs * GRID_W, GRID_W), kr * GRID_W), :]
        outs = []
        for h in range(2):
            qh = jnp.where(masks[h], q, jnp.zeros_like(q))
            sc = _dot_nt(qh, k) + bias_ref[h, var]
            m = jnp.max(sc, axis=-1, keepdims=True)
            e = jnp.exp(sc - m)
            den = jnp.sum(e, axis=-1, keepdims=True)
            outs.append(_dot(e.astype(BF16), v) / den)
        o = jnp.where(masks[0], outs[0], outs[1])
        o_ref[0, pl.ds(pl.multiple_of(r * GRID_W, GRID_W), GRID_W), :] = o.astype(o_ref.dtype)
        return carry

    lax.fori_loop(0, rows, body, 0)


def _na_bias_table(rpb, rows):
    kr = min(NA_ROWS, rows)
    c = jnp.arange(GRID_W)
    kc = jnp.arange(GRID_W)
    cs = jnp.clip(c - NA_COLS // 2, 0, GRID_W - NA_COLS)
    valid = (kc[None, :] >= cs[:, None]) & (kc[None, :] < cs[:, None] + NA_COLS)
    coff = jnp.clip(kc[None, :] - c[:, None] + (NA_COLS - 1), 0, 2 * NA_COLS - 2)
    var = jnp.arange(kr)
    i = jnp.arange(kr)
    roff = i[None, :] - var[:, None] + (NA_ROWS - 1)
    tbl = rpb[:, roff[:, :, None, None], coff[None, None, :, :]]
    tbl = jnp.where(valid[None, None, None], tbl, -jnp.inf)
    tbl = jnp.transpose(tbl, (0, 1, 3, 2, 4))
    return tbl.reshape(rpb.shape[0], kr, GRID_W, kr * GRID_W).astype(F32)


def _na_attention(proj, bias_tbl):
    b, s, _ = proj.shape
    rows = s // GRID_W
    npair = NA_HEADS // 2
    kr = bias_tbl.shape[1]
    blk = lambda off: pl.BlockSpec((1, s, LANES), lambda bi, c: (bi, 0, off + c))
    return pl.pallas_call(
        functools.partial(_na_kernel, rows=rows),
        grid=(b, npair),
        in_specs=[
            blk(0), blk(npair), blk(2 * npair),
            pl.BlockSpec((2, kr, GRID_W, kr * GRID_W), lambda bi, c: (c, 0, 0, 0)),
        ],
        out_specs=pl.BlockSpec((1, s, LANES), lambda bi, c: (bi, 0, c)),
        out_shape=jax.ShapeDtypeStruct((b, s, NA_HEADS * HEAD_DIM), BF16),
        compiler_params=_cparams(("arbitrary", "arbitrary")),
        name="na_attn",
    )(proj, proj, proj, bias_tbl)


def _diff_kernel(lam_ref, q_ref, k_ref, v_ref, g_ref, o_ref, *, tk, lambda_init):
    masks = _head_masks()
    q = q_ref[0]
    tq = q.shape[0]
    s = k_ref.shape[1]
    qs = [jnp.where(masks[m], q, jnp.zeros_like(q)) for m in range(2)]

    def body(j, carry):
        k = k_ref[0, pl.ds(pl.multiple_of(j * tk, tk), tk), :]
        v = v_ref[0, pl.ds(pl.multiple_of(j * tk, tk), tk), :]
        new = []
        for m in range(2):
            m_old, l_old, acc_old = carry[3 * m:3 * m + 3]
            sc = _dot_nt(qs[m], k)
            m_new = jnp.maximum(m_old, jnp.max(sc, axis=-1, keepdims=True))
            alpha = jnp.exp(m_old - m_new)
            e = jnp.exp(sc - m_new)
            l_new = alpha * l_old + jnp.sum(e, axis=-1, keepdims=True)
            acc_new = alpha * acc_old + _dot(e.astype(BF16), v)
            new += [m_new, l_new, acc_new]
        return tuple(new)

    init = []
    for m in range(2):
        init += [jnp.full((tq, 1), -jnp.inf, F32), jnp.zeros((tq, 1), F32), jnp.zeros((tq, LANES), F32)]
    res = lax.fori_loop(0, s // tk, body, tuple(init))
    lam = lam_ref[0]
    o = res[2] / res[1] - lam * (res[5] / res[4])
    o_ref[0] = (_rms(o, g_ref[...]) * (1.0 - lambda_init)).astype(o_ref.dtype)


def _diff_lambda_kernel(lq1_ref, lk1_ref, lq2_ref, lk2_ref, o_ref, *, lambda_init):
    s1 = jnp.sum(lq1_ref[...] * lk1_ref[...], axis=-1, keepdims=True)
    s2 = jnp.sum(lq2_ref[...] * lk2_ref[...], axis=-1, keepdims=True)
    o_ref[...] = jnp.exp(s1) - jnp.exp(s2) + lambda_init


def _diff_attention(proj, lq1, lk1, lq2, lk2, subln_g, lambda_init, col0, tq=256, tk=512):
    b, s, _ = proj.shape
    nh = DIFF_HEADS
    vec = lambda a: a.reshape(1, HEAD_DIM).astype(F32)
    lam = pl.pallas_call(
        functools.partial(_diff_lambda_kernel, lambda_init=lambda_init),
        out_shape=jax.ShapeDtypeStruct((1, 1), F32),
        name="diff_lambda",
    )(vec(lq1), vec(lk1), vec(lq2), vec(lk2)).reshape(1)
    grid_spec = pltpu.PrefetchScalarGridSpec(
        num_scalar_prefetch=1,
        grid=(b, nh, s // tq),
        in_specs=[
            pl.BlockSpec((1, tq, LANES), lambda bi, h, i, lam_r: (bi, i, col0 + h)),
            pl.BlockSpec((1, s, LANES), lambda bi, h, i, lam_r: (bi, 0, col0 + nh + h)),
            pl.BlockSpec((1, s, LANES), lambda bi, h, i, lam_r: (bi, 0, col0 + 2 * nh + h)),
            pl.BlockSpec((1, LANES), lambda bi, h, i, lam_r: (0, 0)),
        ],
        out_specs=pl.BlockSpec((1, tq, LANES), lambda bi, h, i, lam_r: (bi, i, h)),
    )
    return pl.pallas_call(
        functools.partial(_diff_kernel, tk=tk, lambda_init=lambda_init),
        grid_spec=grid_spec,
        out_shape=jax.ShapeDtypeStruct((b, s, nh * LANES), BF16),
        compiler_params=_cparams(("arbitrary", "arbitrary", "arbitrary")),
        name="diff_attn",
    )(lam, proj, proj, proj, subln_g.reshape(1, LANES).astype(F32))


def _dil_kernel(q_ref, k_ref, v_ref, o_ref, lse_ref, *, n_side):
    masks = _head_masks()
    length = q_ref.shape[1]
    tq = DIL_QBLOCK
    tk = tq + 2 * n_side
    rel0 = (lax.broadcasted_iota(jnp.int32, (tq, tk), 1)
            - lax.broadcasted_iota(jnp.int32, (tq, tk), 0))

    def body(i, carry):
        a0 = pl.multiple_of(i * tq, tq)
        ks = pl.multiple_of(jnp.clip(a0 - n_side, 0, length - tk), n_side)
        q = q_ref[0, pl.ds(a0, tq), :]
        k = k_ref[0, pl.ds(ks, tk), :]
        v = v_ref[0, pl.ds(ks, tk), :]
        rel = rel0 + (ks - a0)
        band = (rel >= -n_side) & (rel <= n_side)
        outs, lses = [], []
        for h in range(2):
            qh = jnp.where(masks[h], q, jnp.zeros_like(q))
            sc = jnp.where(band, _dot_nt(qh, k), -jnp.inf)
            m = jnp.max(sc, axis=-1, keepdims=True)
            e = jnp.exp(sc - m)
            den = jnp.sum(e, axis=-1, keepdims=True)
            outs.append(_dot(e.astype(BF16), v) / den)
            lses.append(m + jnp.log(den))
        o_ref[0, pl.ds(a0, tq), :] = jnp.where(masks[0], outs[0], outs[1]).astype(o_ref.dtype)
        lse_ref[0, 0, 0, pl.ds(a0, tq), :] = _columns(lses)
        return carry

    lax.fori_loop(0, length // tq, body, 0)


def _dil_attention(proj, group, window, dil):
    b, s, n_all = proj.shape
    n_side = (window // 2) // dil
    length = s // dil
    npair = DIL_HEADS // 2
    cw = DIL_HEADS * HEAD_DIM
    tiles_all = n_all // LANES
    base = group * 3 * npair
    view = proj.reshape(b, length, dil * n_all)
    blk = lambda off: pl.BlockSpec(
        (1, length, LANES), lambda bi, c, r: (bi, 0, r * tiles_all + base + off + c))
    o, lse = pl.pallas_call(
        functools.partial(_dil_kernel, n_side=n_side),
        grid=(b, npair, dil),
        in_specs=[blk(0), blk(npair), blk(2 * npair)],
        out_specs=[
            pl.BlockSpec((1, length, LANES), lambda bi, c, r: (bi, 0, r * npair + c)),
            pl.BlockSpec((1, 1, 1, length, 2), lambda bi, c, r: (c, bi, r, 0, 0)),
        ],
        out_shape=[
            jax.ShapeDtypeStruct((b, length, dil * cw), BF16),
            jax.ShapeDtypeStruct((npair, b, dil, length, 2), F32),
        ],
        compiler_params=_cparams(("arbitrary", "arbitrary", "arbitrary")),
        name="dil_attn",
    )(view, view, view)
    lse = jnp.transpose(lse, (0, 1, 3, 2, 4)).reshape(npair, b, s, 2)
    return o.reshape(b, s, cw), lse


def _dil_mix_kernel(o0_ref, o1_ref, o2_ref, lse_ref, o_ref):
    masks = _head_masks()
    lse = lse_ref[:, 0, 0]
    w = jnp.exp(lse - jnp.max(lse, axis=0, keepdims=True))
    w = w / jnp.sum(w, axis=0, keepdims=True)
    acc = None
    for g, ref in enumerate((o0_ref, o1_ref, o2_ref)):
        wg = jnp.where(masks[0], w[g, :, 0:1], w[g, :, 1:2])
        term = wg * ref[0].astype(F32)
        acc = term if acc is None else acc + term
    o_ref[0] = acc.astype(o_ref.dtype)


def _dil_mix(outs, lses, tt=1024):
    b, s, cw = outs[0].shape
    npair = cw // LANES
    lse = jnp.stack(lses, axis=0)
    ng = lse.shape[0]
    oblk = pl.BlockSpec((1, tt, LANES), lambda bi, i, c: (bi, i, c))
    return pl.pallas_call(
        _dil_mix_kernel,
        grid=(b, s // tt, npair),
        in_specs=[oblk, oblk, oblk,
                  pl.BlockSpec((ng, 1, 1, tt, 2), lambda bi, i, c: (0, c, bi, i, 0))],
        out_specs=oblk,
        out_shape=jax.ShapeDtypeStruct((b, s, cw), BF16),
        compiler_params=_cparams(("arbitrary", "arbitrary", "arbitrary")),
        name="dil_mix",
    )(outs[0], outs[1], outs[2], lse)


def _norm_router_kernel(x_ref, g_ref, scale_ref, shift_ref, rw_ref, rb_ref, h_ref, idx_ref, gate_ref):
    h = _rms(x_ref[0], g_ref[...]) * (1.0 + scale_ref[0]) + shift_ref[0]
    h_ref[0] = h
    logits = _dot3(h, rw_ref[...]) + rb_ref[...]
    n_e = logits.shape[1]
    eidx = lax.broadcasted_iota(jnp.int32, logits.shape, 1).astype(F32)
    vals, idxs = [], []
    for _ in range(TOP_K):
        m = jnp.max(logits, axis=-1, keepdims=True)
        first = jnp.min(jnp.where(logits == m, eidx, float(n_e)), axis=-1, keepdims=True)
        vals.append(m)
        idxs.append(first)
        logits = jnp.where(eidx == first, -jnp.inf, logits)
    top = _columns(vals)
    e = jnp.exp(top - vals[0])
    gate_ref[0] = e / jnp.sum(e, axis=-1, keepdims=True)
    idx_ref[0] = _columns(idxs).astype(jnp.int32)


def _norm_router(x, g, scale, shift, rw, rb, tm=512):
    b, s, d = x.shape
    n_e = rw.shape[1]
    tok = lambda w: pl.BlockSpec((1, tm, w), lambda bi, i: (bi, i, 0))
    return pl.pallas_call(
        _norm_router_kernel,
        grid=(b, s // tm),
        in_specs=[
            tok(d),
            pl.BlockSpec((1, d), lambda bi, i: (0, 0)),
            pl.BlockSpec((1, 1, d), lambda bi, i: (bi, 0, 0)),
            pl.BlockSpec((1, 1, d), lambda bi, i: (bi, 0, 0)),
            pl.BlockSpec((d, n_e), lambda bi, i: (0, 0)),
            pl.BlockSpec((1, n_e), lambda bi, i: (0, 0)),
        ],
        out_specs=[tok(d), tok(TOP_K), tok(TOP_K)],
        out_shape=[
            jax.ShapeDtypeStruct((b, s, d), F32),
            jax.ShapeDtypeStruct((b, s, TOP_K), jnp.int32),
            jax.ShapeDtypeStruct((b, s, TOP_K), F32),
        ],
        compiler_params=_cparams(("arbitrary", "arbitrary")),
        name="norm_router",
    )(x, g.reshape(1, d), scale, shift, rw, rb.reshape(1, n_e))


def _moe_plan(top_idx, n_tok):
    n_assign = n_tok * TOP_K
    flat_e = top_idx.reshape(-1)
    onehot = (flat_e[:, None] == jnp.arange(N_EXPERTS, dtype=jnp.int32)[None, :]).astype(jnp.int32)
    csum = jnp.cumsum(onehot, axis=0)
    counts = csum[-1]
    rank = jnp.take_along_axis(csum, flat_e[:, None], axis=1)[:, 0] - 1
    padded = (counts + MOE_SUB - 1) // MOE_SUB * MOE_SUB
    pstarts = jnp.cumsum(padded) - padded
    dest = pstarts[flat_e] + rank
    n_slots = n_assign + N_EXPERTS * MOE_SUB
    slot_tok = jnp.zeros((n_slots,), jnp.int32).at[dest].set(
        jnp.arange(n_assign, dtype=jnp.int32) // TOP_K)
    slot_dst = jnp.full((n_slots,), -1, jnp.int32).at[dest].set(jnp.arange(n_assign, dtype=jnp.int32))
    n_items = (n_slots + N_EXPERTS * (MOE_ITEM_ROWS - MOE_SUB)) // MOE_ITEM_ROWS
    items_per_e = (padded + MOE_ITEM_ROWS - 1) // MOE_ITEM_ROWS
    item_ends = jnp.cumsum(items_per_e)
    total_items = item_ends[-1]
    i = jnp.arange(n_items, dtype=jnp.int32)
    i_eff = jnp.minimum(i, total_items - 1)
    e_i = jnp.clip(jnp.searchsorted(item_ends, i_eff, side='right'), 0, N_EXPERTS - 1).astype(jnp.int32)
    local = i_eff - (item_ends[e_i] - items_per_e[e_i])
    row0 = pstarts[e_i] + local * MOE_ITEM_ROWS
    nrows = jnp.clip(padded[e_i] - local * MOE_ITEM_ROWS, 0, MOE_ITEM_ROWS)
    nsub = jnp.where(i < total_items, nrows // MOE_SUB, 0)
    return (e_i.astype(jnp.int32), row0.astype(jnp.int32), nsub.astype(jnp.int32),
            slot_tok, slot_dst)


def _swiglu_pair(h_a, h_b):
    n = h_a.shape[1]
    lane = lax.broadcasted_iota(jnp.int32, (1, n), 1)
    even = (lane % 2) == 0

    def glu_lin(h):
        glu = jnp.minimum(h, SWIGLU_LIMIT)
        glu = glu * jax.nn.sigmoid(SWIGLU_ALPHA * glu)
        lin = jnp.clip(h, -SWIGLU_LIMIT, SWIGLU_LIMIT) + 1.0
        return glu, lin

    glu_a, lin_a = glu_lin(h_a)
    glu_b, lin_b = glu_lin(h_b)
    act_a = glu_a * pltpu.roll(lin_a, n - 1, 1)
    act_b = pltpu.roll(glu_b, 1, 1) * lin_b
    return jnp.where(even, act_a, act_b)


def _moe_kernel(ie_ref, row0_ref, nsub_ref, stok_ref, sdst_ref,
                h_hbm, wgu_a_ref, wgu_b_ref, bgu_a_ref, bgu_b_ref, wdn_a_ref, wdn_b_ref, bdn_ref,
                y_hbm,
                xs_ref, acc_ref, gbuf_ref, obuf_ref, wgu_ref, wdn_il_ref, wdn_ref, gsem, ssem):
    i = pl.program_id(0)
    f = pl.program_id(1)
    nf = pl.num_programs(1)
    nsub = nsub_ref[i]
    row0 = row0_ref[i]
    d = xs_ref.shape[1]
    ft2 = 2 * MOE_FT

    def gather_copy(slot, r):
        return pltpu.make_async_copy(h_hbm.at[stok_ref[slot]],
                                     gbuf_ref.at[pl.ds(r * ROW_SLAB, ROW_SLAB), :], gsem)

    def scatter_copy(dst, r):
        return pltpu.make_async_copy(obuf_ref.at[pl.ds(r * ROW_SLAB, ROW_SLAB), :],
                                     y_hbm.at[dst], ssem)

    @pl.when((f == 0) & (nsub > 0))
    def _gather():
        def per_sub(sb, carry):
            base = row0 + sb * MOE_SUB

            def start(r, c):
                gather_copy(base + r, r).start()
                return c

            def wait(r, c):
                gather_copy(base + r, r).wait()
                return c

            lax.fori_loop(0, MOE_SUB, start, 0)
            lax.fori_loop(0, MOE_SUB, wait, 0)
            for jc in range(ROW_SLAB):
                chunk = gbuf_ref[pl.ds(jc, MOE_SUB, stride=ROW_SLAB), :]
                xs_ref[pl.ds(pl.multiple_of(sb * MOE_SUB, MOE_SUB), MOE_SUB),
                       jc * LANES:(jc + 1) * LANES] = chunk.astype(BF16)
            return carry

        lax.fori_loop(0, nsub, per_sub, 0)

    @pl.when(nsub > 0)
    def _compute():
        wgu_ref[:, 0:ft2] = wgu_a_ref[0].astype(BF16)
        wgu_ref[:, ft2:2 * ft2] = wgu_b_ref[0].astype(BF16)
        for jc in range(ROW_SLAB):
            cols = slice(jc * LANES, (jc + 1) * LANES)
            wdn_il_ref[jc, pl.ds(0, MOE_FT, stride=2), :] = wdn_a_ref[0, :, cols]
            wdn_il_ref[jc, pl.ds(1, MOE_FT, stride=2), :] = wdn_b_ref[0, :, cols]
            wdn_ref[:, cols] = wdn_il_ref[jc].astype(BF16)
        b_a = bgu_a_ref[0]
        b_b = bgu_b_ref[0]

        def per_sub(sb, carry):
            rows = pl.ds(pl.multiple_of(sb * MOE_SUB, MOE_SUB), MOE_SUB)
            hgu = _dot(xs_ref[rows, :], wgu_ref[...])
            act = _swiglu_pair(hgu[:, 0:ft2] + b_a, hgu[:, ft2:2 * ft2] + b_b)
            contrib = _dot(act.astype(BF16), wdn_ref[...])

            @pl.when(f == 0)
            def _():
                acc_ref[rows, :] = contrib

            @pl.when(f != 0)
            def _():
                acc_ref[rows, :] += contrib

            return carry

        lax.fori_loop(0, nsub, per_sub, 0)

    @pl.when((f == nf - 1) & (nsub > 0))
    def _scatter():
        def per_sub(sb, carry):
            base = row0 + sb * MOE_SUB
            out = acc_ref[pl.ds(pl.multiple_of(sb * MOE_SUB, MOE_SUB), MOE_SUB), :] + bdn_ref[0]
            for jc in range(ROW_SLAB):
                obuf_ref[pl.ds(jc, MOE_SUB, stride=ROW_SLAB), :] = out[:, jc * LANES:(jc + 1) * LANES]

            def start(r, c):
                dst = sdst_ref[base + r]

                @pl.when(dst >= 0)
                def _():
                    scatter_copy(dst, r).start()

                return c

            def wait(r, c):
                dst = sdst_ref[base + r]

                @pl.when(dst >= 0)
                def _():
                    scatter_copy(dst, r).wait()

                return c

            lax.fori_loop(0, MOE_SUB, start, 0)
            lax.fori_loop(0, MOE_SUB, wait, 0)
            return carry

        lax.fori_loop(0, nsub, per_sub, 0)


def _moe_experts(h, plan, w_gu, b_gu, w_dn, b_dn):
    n_tok, d = h.shape
    n_e, _, two_f = w_gu.shape
    d_ff = two_f // 2
    item_e, item_row0, item_nsub, slot_tok, slot_dst = plan
    n_items = item_e.shape[0]
    nf = (d_ff // 2) // MOE_FT
    ft2 = 2 * MOE_FT
    half_gu = (two_f // 2) // ft2
    half_dn = (d_ff // 2) // MOE_FT
    slabs = d // LANES
    assert slabs == ROW_SLAB

    def f_eff(i, f, nsub):
        return jnp.where(nsub[i] > 0, f, nf - 1)

    def wgu_map(half):
        return lambda i, f, ie, r0, ns, st, sd: (ie[i], 0, half * half_gu + f_eff(i, f, ns))

    def wdn_map(half):
        return lambda i, f, ie, r0, ns, st, sd: (ie[i], half * half_dn + f_eff(i, f, ns), 0)

    grid_spec = pltpu.PrefetchScalarGridSpec(
        num_scalar_prefetch=5,
        grid=(n_items, nf),
        in_specs=[
            pl.BlockSpec(memory_space=pl.ANY),
            pl.BlockSpec((1, d, ft2), wgu_map(0)),
            pl.BlockSpec((1, d, ft2), wgu_map(1)),
            pl.BlockSpec((1, 1, ft2), wgu_map(0)),
            pl.BlockSpec((1, 1, ft2), wgu_map(1)),
            pl.BlockSpec((1, MOE_FT, d), wdn_map(0)),
            pl.BlockSpec((1, MOE_FT, d), wdn_map(1)),
            pl.BlockSpec((1, 1, d), lambda i, f, ie, r0, ns, st, sd: (ie[i], 0, 0)),
        ],
        out_specs=pl.BlockSpec(memory_space=pl.ANY),
        scratch_shapes=[
            pltpu.VMEM((MOE_ITEM_ROWS, d), BF16),
            pltpu.VMEM((MOE_ITEM_ROWS, d), F32),
            pltpu.VMEM((MOE_SUB * ROW_SLAB, LANES), F32),
            pltpu.VMEM((MOE_SUB * ROW_SLAB, LANES), F32),
            pltpu.VMEM((d, 2 * ft2), BF16),
            pltpu.VMEM((ROW_SLAB, ft2, LANES), F32),
            pltpu.VMEM((ft2, d), BF16),
            pltpu.SemaphoreType.DMA(()),
            pltpu.SemaphoreType.DMA(()),
        ],
    )
    y = pl.pallas_call(
        _moe_kernel,
        grid_spec=grid_spec,
        out_shape=jax.ShapeDtypeStruct((n_tok * TOP_K, ROW_SLAB, LANES), F32),
        compiler_params=_cparams(("arbitrary", "arbitrary")),
        name="moe_experts",
    )(item_e, item_row0, item_nsub, slot_tok, slot_dst,
      h.reshape(n_tok, ROW_SLAB, LANES), w_gu, w_gu,
      b_gu.reshape(n_e, 1, two_f), b_gu.reshape(n_e, 1, two_f),
      w_dn, w_dn, b_dn.reshape(n_e, 1, d))
    return y


def _moe_combine_kernel(x_ref, y_ref, g_ref, gate_ref, o_ref):
    gates = g_ref[0]
    for jc in range(ROW_SLAB):
        tot = None
        for k in range(TOP_K):
            term = gates[:, k:k + 1] * y_ref[0, :, k * ROW_SLAB + jc, :]
            tot = term if tot is None else tot + term
        cols = slice(jc * LANES, (jc + 1) * LANES)
        o_ref[0, :, cols] = x_ref[0, :, cols] + gate_ref[0, :, cols] * tot


def _moe_combine(x, y, gates, gate_mod, tt=256):
    b, s, d = x.shape
    y4 = y.reshape(b, s, TOP_K * ROW_SLAB, LANES)
    return pl.pallas_call(
        _moe_combine_kernel,
        grid=(b, s // tt),
        in_specs=[
            pl.BlockSpec((1, tt, d), lambda bi, i: (bi, i, 0)),
            pl.BlockSpec((1, tt, TOP_K * ROW_SLAB, LANES), lambda bi, i: (bi, i, 0, 0)),
            pl.BlockSpec((1, tt, TOP_K), lambda bi, i: (bi, i, 0)),
            pl.BlockSpec((1, 1, d), lambda bi, i: (bi, 0, 0)),
        ],
        out_specs=pl.BlockSpec((1, tt, d), lambda bi, i: (bi, i, 0)),
        out_shape=jax.ShapeDtypeStruct((b, s, d), F32),
        compiler_params=_cparams(("arbitrary", "arbitrary")),
        name="moe_combine",
    )(x, y4, gates, gate_mod)


def _moe_ffn(x, g, scale, shift, gate_mod, rw, rb, w_gu, b_gu, w_dn, b_dn):
    b, s, d = x.shape
    h, top_idx, gates = _norm_router(x, g, scale, shift, rw, rb)
    plan = _moe_plan(top_idx.reshape(b * s, TOP_K), b * s)
    y = _moe_experts(h.reshape(b * s, d), plan, w_gu, b_gu, w_dn, b_dn)
    return _moe_combine(x, y, gates, gate_mod)


def _rope_lane_tables(seq):
    inv_freq = ROPE_THETA ** (-jnp.arange(0, HEAD_DIM, 2, dtype=F32) / HEAD_DIM)
    ang = jnp.arange(seq, dtype=F32)[:, None] * inv_freq[None, :]
    cos, sin = jnp.cos(ang), jnp.sin(ang)
    reps = LANES // HEAD_DIM
    return (jnp.tile(jnp.concatenate([cos, cos], axis=-1), (1, reps)),
            jnp.tile(jnp.concatenate([-sin, sin], axis=-1), (1, reps)))


def _col_modes(kinds, tn, width):
    modes = []
    for n_cols, mode in kinds:
        assert n_cols % tn == 0
        modes += [mode] * (n_cols // tn)
    assert len(modes) * tn == width
    return jnp.asarray(modes, jnp.int32)


PLAIN, ROPE, SCALE, ROPE_SCALE = 0, 1, 2, 3


def kernel(x, c, ada_w, ada_b, norm_g, final_g, ab_w_in, ab_w_out, na_rpb, diff_lam_q1, diff_lam_k1,
           diff_lam_q2, diff_lam_k2, diff_subln_g, dil_w_in, dil_w_out, moe_router_w, moe_router_b,
           moe_w_gate_up, moe_b_gate_up, moe_w_down, moe_b_down):
    b, s, d = x.shape
    depth = ada_w.shape[0]
    tn = 512
    cos_t, sin_t = _rope_lane_tables(s)

    c_pad = jnp.zeros((8, d), F32).at[:b].set(c)
    mod = _adaln(c_pad, ada_w.reshape(depth * 2, d, 3 * d), ada_b.reshape(depth * 2, 3 * d))
    mod = mod[:, :b].reshape(depth, 2, b, 3, 1, d)

    na_w = NA_HEADS * HEAD_DIM
    df_w = DIFF_HEADS * 2 * HEAD_DIM
    dil_w = DIL_HEADS * HEAD_DIM

    for l in range(depth):
        shift, scale, gate = (mod[l, 0, :, t] for t in range(3))
        h = _norm_mod(x, norm_g[l, 0], scale, shift, BF16)
        i = l // 2
        if l % 2 == 0:
            lambda_init = 0.8 - 0.6 * math.exp(-0.3 * l)
            modes = _col_modes([(na_w, SCALE), (2 * na_w, PLAIN), (df_w, ROPE_SCALE), (df_w, ROPE),
                                (df_w, PLAIN)], tn, ab_w_in.shape[2])
            proj = _proj(h, ab_w_in[i], modes, cos_t, sin_t, tn=tn)
            o_na = _na_attention(proj, _na_bias_table(na_rpb[i], s // GRID_W))
            o_df = _diff_attention(proj, diff_lam_q1[i], diff_lam_k1[i], diff_lam_q2[i], diff_lam_k2[i],
                                   diff_subln_g[i], lambda_init, col0=3 * na_w // LANES)
            o = jnp.concatenate([o_na, o_df], axis=-1)
            x = _outproj(o, ab_w_out[i], x, gate)
        else:
            modes = _col_modes([(dil_w, ROPE_SCALE), (dil_w, ROPE), (dil_w, PLAIN)] * len(DIL_CONFIGS),
                               tn, dil_w_in.shape[2])
            proj = _proj(h, dil_w_in[i], modes, cos_t, sin_t, tn=tn)
            outs, lses = [], []
            for gi, (window, dil) in enumerate(DIL_CONFIGS):
                o_g, lse_g = _dil_attention(proj, gi, window, dil)
                outs.append(o_g)
                lses.append(lse_g)
            o = _dil_mix(outs, lses)
            x = _outproj(o, dil_w_out[i], x, gate)
        shift, scale, gate = (mod[l, 1, :, t] for t in range(3))
        x = _moe_ffn(x, norm_g[l, 1], scale, shift, gate, moe_router_w[l], moe_router_b[l],
                     moe_w_gate_up[l], moe_b_gate_up[l], moe_w_down[l], moe_b_down[l])
    return _final_norm(x, final_g)
```

```python
import functools
import math

import jax
import jax.numpy as jnp
from jax import lax
from jax.experimental import pallas as pl
from jax.experimental.pallas import tpu as pltpu

F32 = jnp.float32
BF16 = jnp.bfloat16

HEAD_DIM = 64
LANES = 128
GRID_W = 64
NA_HEADS = 16
NA_ROWS = 8
NA_COLS = 16
DIFF_HEADS = 8
DIL_HEADS = 32
DIL_CONFIGS = ((128, 1), (512, 4), (2048, 16))
ROPE_THETA = 10000.0
N_EXPERTS = 32
TOP_K = 4
SWIGLU_LIMIT = 7.0
SWIGLU_ALPHA = 1.702
RMS_EPS = 1e-5

DIL_QBLOCK = 128
MOE_SUB = 256
MOE_NSUB = 5
MOE_ITEM_ROWS = MOE_NSUB * MOE_SUB
MOE_FT = 128
MOE_TILE = 2 * MOE_FT
DISPATCH_CHUNK = 512
DMA_UNROLL = 8
COMBINE_TOKENS = 128
VMEM_LIMIT = 56 * 1024 * 1024


def _cparams(sem, vmem=VMEM_LIMIT):
    return pltpu.CompilerParams(dimension_semantics=sem, vmem_limit_bytes=vmem)


def _dot(a, b):
    return jnp.dot(a, b, preferred_element_type=F32)


def _dot_nt(a, b):
    return lax.dot_general(a, b, (((1,), (1,)), ((), ())), preferred_element_type=F32)


def _split_bf16(a):
    hi = a.astype(BF16)
    lo = (a - hi.astype(F32)).astype(BF16)
    return hi, lo


def _dot3(a, b):
    ah, al = _split_bf16(a)
    bh, bl = _split_bf16(b)
    return _dot(ah, bh) + (_dot(ah, bl) + _dot(al, bh))


def _adaln_kernel(c_ref, w_ref, b_ref, o_ref):
    c = c_ref[...]
    c_act = c * jax.nn.sigmoid(c)
    o_ref[0] = _dot3(c_act, w_ref[0]) + b_ref[0]


def _adaln(c_pad, w, b, tn=768):
    n_mod, d, n = w.shape
    return pl.pallas_call(
        _adaln_kernel,
        grid=(n_mod, n // tn),
        in_specs=[
            pl.BlockSpec(c_pad.shape, lambda l, j: (0, 0)),
            pl.BlockSpec((1, d, tn), lambda l, j: (l, 0, j)),
            pl.BlockSpec((1, 1, tn), lambda l, j: (l, 0, j)),
        ],
        out_specs=pl.BlockSpec((1, c_pad.shape[0], tn), lambda l, j: (l, 0, j)),
        out_shape=jax.ShapeDtypeStruct((n_mod, c_pad.shape[0], n), F32),
        compiler_params=_cparams(("arbitrary", "arbitrary")),
        name="adaln",
    )(c_pad, w, b.reshape(n_mod, 1, n))


def _rms(x, g):
    return x * lax.rsqrt(jnp.mean(x * x, axis=-1, keepdims=True) + RMS_EPS) * g


def _norm_mod_kernel(x_ref, g_ref, scale_ref, shift_ref, o_ref):
    h = _rms(x_ref[0], g_ref[...]) * (1.0 + scale_ref[0]) + shift_ref[0]
    o_ref[0] = h.astype(o_ref.dtype)


def _norm_mod(x, g, scale, shift, out_dtype, tm=512):
    b, s, d = x.shape
    return pl.pallas_call(
        _norm_mod_kernel,
        grid=(b, s // tm),
        in_specs=[
            pl.BlockSpec((1, tm, d), lambda bi, i: (bi, i, 0)),
            pl.BlockSpec((1, d), lambda bi, i: (0, 0)),
            pl.BlockSpec((1, 1, d), lambda bi, i: (bi, 0, 0)),
            pl.BlockSpec((1, 1, d), lambda bi, i: (bi, 0, 0)),
        ],
        out_specs=pl.BlockSpec((1, tm, d), lambda bi, i: (bi, i, 0)),
        out_shape=jax.ShapeDtypeStruct((b, s, d), out_dtype),
        compiler_params=_cparams(("arbitrary", "arbitrary")),
        name="norm_mod",
    )(x, g.reshape(1, d), scale, shift)


def _final_norm_kernel(x_ref, g_ref, o_ref):
    o_ref[0] = _rms(x_ref[0], g_ref[...])


def _final_norm(x, g, tm=512):
    b, s, d = x.shape
    return pl.pallas_call(
        _final_norm_kernel,
        grid=(b, s // tm),
        in_specs=[
            pl.BlockSpec((1, tm, d), lambda bi, i: (bi, i, 0)),
            pl.BlockSpec((1, d), lambda bi, i: (0, 0)),
        ],
        out_specs=pl.BlockSpec((1, tm, d), lambda bi, i: (bi, i, 0)),
        out_shape=jax.ShapeDtypeStruct((b, s, d), F32),
        compiler_params=_cparams(("arbitrary", "arbitrary")),
        name="final_norm",
    )(x, g.reshape(1, d))


def _proj_kernel(mode_ref, a_ref, w_ref, cos_ref, sin_ref, o_ref, wbf_ref, stage_ref, *, jbase, dil):
    j = pl.program_id(0)

    @pl.when((pl.program_id(1) == 0) & (pl.program_id(2) == 0))
    def _():
        wbf_ref[...] = w_ref[...].astype(BF16)

    acc = _dot(a_ref[0], wbf_ref[...])
    mode = mode_ref[jbase + j]
    tm, tn = acc.shape

    def emit(ci, val):
        cols = slice(ci * LANES, (ci + 1) * LANES)
        if dil == 1:
            o_ref[0, 0, :, cols] = val.astype(o_ref.dtype)
        else:
            stage_ref[ci] = val
            for r in range(dil):
                o_ref[0, r, :, cols] = stage_ref[ci, pl.ds(r, tm // dil, stride=dil), :].astype(o_ref.dtype)

    @pl.when(mode == 0)
    def _():
        for ci in range(tn // LANES):
            emit(ci, acc[:, ci * LANES:(ci + 1) * LANES])

    @pl.when(mode != 0)
    def _():
        rope_on = (mode & 1).astype(F32)
        qscale = jnp.where((mode & 2) == 2, HEAD_DIM ** -0.5, 1.0).astype(F32)
        lane = lax.broadcasted_iota(jnp.int32, (1, LANES), 1)
        first_half = (lane % HEAD_DIM) < (HEAD_DIM // 2)
        cos = (cos_ref[...] * rope_on + (1.0 - rope_on)) * qscale
        sin = sin_ref[...] * (rope_on * qscale)
        for ci in range(tn // LANES):
            x = acc[:, ci * LANES:(ci + 1) * LANES]
            partner = jnp.where(first_half,
                                pltpu.roll(x, LANES - HEAD_DIM // 2, 1),
                                pltpu.roll(x, HEAD_DIM // 2, 1))
            emit(ci, x * cos + partner * sin)


def _proj(a, w, modes, cos_t, sin_t, n_out, jbase=0, dil=1, tm=1024, tn=512):
    b, s, k = a.shape
    grid_spec = pltpu.PrefetchScalarGridSpec(
        num_scalar_prefetch=1,
        grid=(n_out // tn, b, s // tm),
        in_specs=[
            pl.BlockSpec((1, tm, k), lambda j, bi, i, m: (bi, i, 0)),
            pl.BlockSpec((k, tn), lambda j, bi, i, m: (0, jbase + j)),
            pl.BlockSpec((tm, LANES), lambda j, bi, i, m: (i, 0)),
            pl.BlockSpec((tm, LANES), lambda j, bi, i, m: (i, 0)),
        ],
        out_specs=pl.BlockSpec((1, dil, tm // dil, tn), lambda j, bi, i, m: (bi, 0, i, j)),
        scratch_shapes=[pltpu.VMEM((k, tn), BF16), pltpu.VMEM((tn // LANES, tm, LANES), F32)],
    )
    return pl.pallas_call(
        functools.partial(_proj_kernel, jbase=jbase, dil=dil),
        grid_spec=grid_spec,
        out_shape=jax.ShapeDtypeStruct((b, dil, s // dil, n_out), BF16),
        compiler_params=_cparams(("arbitrary", "arbitrary", "arbitrary")),
        name="proj",
    )(modes, a, w, cos_t, sin_t)


def _outproj_kernel(a_ref, w_ref, x_ref, gate_ref, o_ref, wbf_ref):
    @pl.when((pl.program_id(1) == 0) & (pl.program_id(2) == 0))
    def _():
        wbf_ref[...] = w_ref[...].astype(BF16)

    o_ref[0] = x_ref[0] + gate_ref[0] * _dot(a_ref[0], wbf_ref[...])


def _outproj(a, w, x, gate, tm=1024, tn=512):
    b, s, k = a.shape
    n = w.shape[1]
    return pl.pallas_call(
        _outproj_kernel,
        grid=(n // tn, b, s // tm),
        in_specs=[
            pl.BlockSpec((1, tm, k), lambda j, bi, i: (bi, i, 0)),
            pl.BlockSpec((k, tn), lambda j, bi, i: (0, j)),
            pl.BlockSpec((1, tm, tn), lambda j, bi, i: (bi, i, j)),
            pl.BlockSpec((1, 1, tn), lambda j, bi, i: (bi, 0, j)),
        ],
        out_specs=pl.BlockSpec((1, tm, tn), lambda j, bi, i: (bi, i, j)),
        out_shape=jax.ShapeDtypeStruct((b, s, n), F32),
        scratch_shapes=[pltpu.VMEM((k, tn), BF16)],
        compiler_params=_cparams(("arbitrary", "arbitrary", "arbitrary")),
        name="outproj",
    )(a, w, x, gate)


def _columns(cols):
    m = cols[0].shape[0]
    lane = lax.broadcasted_iota(jnp.int32, (m, len(cols)), 1)
    out = jnp.broadcast_to(cols[0], (m, len(cols)))
    for k in range(1, len(cols)):
        out = jnp.where(lane == k, cols[k], out)
    return out


def _head_masks():
    lane = lax.broadcasted_iota(jnp.int32, (1, LANES), 1)
    return lane < HEAD_DIM, lane >= HEAD_DIM


def _na_kernel(q_ref, k_ref, v_ref, bias_ref, o_ref, *, rows):
    kr = min(NA_ROWS, rows)
    masks = _head_masks()

    def body(r, carry):
        rs = jnp.clip(r - kr // 2, 0, rows - kr)
        var = r - rs
        q = q_ref[0, pl.ds(pl.multiple_of(r * GRID_W, GRID_W), GRID_W), :]
        k = k_ref[0, pl.ds(pl.multiple_of(rs * GRID_W, GRID_W), kr * GRID_W), :]
        v = v_ref[0, pl.ds(pl.multiple_of(rs * GRID_W, GRID_W), kr * GRID_W), :]
        outs = []
        for h in range(2):
            qh = jnp.where(masks[h], q, jnp.zeros_like(q))
            sc = _dot_nt(qh, k) + bias_ref[h, var]
            m = jnp.max(sc, axis=-1, keepdims=True)
            e = jnp.exp(sc - m)
            den = jnp.sum(e, axis=-1, keepdims=True)
            outs.append(_dot(e.astype(BF16), v) / den)
        o = jnp.where(masks[0], outs[0], outs[1])
        o_ref[0, pl.ds(pl.multiple_of(r * GRID_W, GRID_W), GRID_W), :] = o.astype(o_ref.dtype)
        return carry

    lax.fori_loop(0, rows, body, 0)


def _na_bias_table(rpb, rows):
    kr = min(NA_ROWS, rows)
    c = jnp.arange(GRID_W)
    kc = jnp.arange(GRID_W)
    cs = jnp.clip(c - NA_COLS // 2, 0, GRID_W - NA_COLS)
    valid = (kc[None, :] >= cs[:, None]) & (kc[None, :] < cs[:, None] + NA_COLS)
    coff = jnp.clip(kc[None, :] - c[:, None] + (NA_COLS - 1), 0, 2 * NA_COLS - 2)
    rows_v = jnp.stack([rpb[:, NA_ROWS - 1 - v:NA_ROWS - 1 - v + kr, :] for v in range(kr)], axis=1)
    onehot = ((coff[None] == jnp.arange(2 * NA_COLS - 1)[:, None, None]) & valid[None]).astype(F32)
    tbl = jnp.einsum('hvio,ock->hvcik', rows_v.astype(F32), onehot, precision=lax.Precision.HIGHEST)
    tbl = jnp.where(valid[None, None, :, None, :], tbl, -jnp.inf)
    return tbl.reshape(rpb.shape[0], kr, GRID_W, kr * GRID_W)


def _na_attention(proj, bias_tbl):
    b, s, _ = proj.shape
    rows = s // GRID_W
    npair = NA_HEADS // 2
    kr = bias_tbl.shape[1]
    blk = lambda off: pl.BlockSpec((1, s, LANES), lambda bi, c: (bi, 0, off + c))
    return pl.pallas_call(
        functools.partial(_na_kernel, rows=rows),
        grid=(b, npair),
        in_specs=[
            blk(0), blk(npair), blk(2 * npair),
            pl.BlockSpec((2, kr, GRID_W, kr * GRID_W), lambda bi, c: (c, 0, 0, 0)),
        ],
        out_specs=pl.BlockSpec((1, s, LANES), lambda bi, c: (bi, 0, c)),
        out_shape=jax.ShapeDtypeStruct((b, s, NA_HEADS * HEAD_DIM), BF16),
        compiler_params=_cparams(("arbitrary", "arbitrary")),
        name="na_attn",
    )(proj, proj, proj, bias_tbl)


def _diff_kernel(lam_ref, q_ref, k_ref, v_ref, g_ref, o_ref, *, tk, lambda_init):
    masks = _head_masks()
    q = q_ref[0]
    tq = q.shape[0]
    s = k_ref.shape[1]
    qs = [jnp.where(masks[m], q, jnp.zeros_like(q)) for m in range(2)]

    def body(j, carry):
        k = k_ref[0, pl.ds(pl.multiple_of(j * tk, tk), tk), :]
        v = v_ref[0, pl.ds(pl.multiple_of(j * tk, tk), tk), :]
        new = []
        for m in range(2):
            m_old, l_old, acc_old = carry[3 * m:3 * m + 3]
            sc = _dot_nt(qs[m], k)
            m_new = jnp.maximum(m_old, jnp.max(sc, axis=-1, keepdims=True))
            alpha = jnp.exp(m_old - m_new)
            e = jnp.exp(sc - m_new)
            l_new = alpha * l_old + jnp.sum(e, axis=-1, keepdims=True)
            acc_new = alpha * acc_old + _dot(e.astype(BF16), v)
            new += [m_new, l_new, acc_new]
        return tuple(new)

    init = []
    for m in range(2):
        init += [jnp.full((tq, 1), -jnp.inf, F32), jnp.zeros((tq, 1), F32), jnp.zeros((tq, LANES), F32)]
    res = lax.fori_loop(0, s // tk, body, tuple(init))
    lam = lam_ref[0]
    o = res[2] / res[1] - lam * (res[5] / res[4])
    o_ref[0] = (_rms(o, g_ref[...]) * (1.0 - lambda_init)).astype(o_ref.dtype)


def _diff_lambda_kernel(lq1_ref, lk1_ref, lq2_ref, lk2_ref, o_ref, *, lambda_init):
    s1 = jnp.sum(lq1_ref[...] * lk1_ref[...], axis=-1, keepdims=True)
    s2 = jnp.sum(lq2_ref[...] * lk2_ref[...], axis=-1, keepdims=True)
    o_ref[...] = jnp.exp(s1) - jnp.exp(s2) + lambda_init


def _diff_attention(proj, lq1, lk1, lq2, lk2, subln_g, lambda_init, col0, tq=256, tk=512):
    b, s, _ = proj.shape
    nh = DIFF_HEADS
    vec = lambda a: a.reshape(1, HEAD_DIM).astype(F32)
    lam = pl.pallas_call(
        functools.partial(_diff_lambda_kernel, lambda_init=lambda_init),
        out_shape=jax.ShapeDtypeStruct((1, 1), F32),
        name="diff_lambda",
    )(vec(lq1), vec(lk1), vec(lq2), vec(lk2)).reshape(1)
    grid_spec = pltpu.PrefetchScalarGridSpec(
        num_scalar_prefetch=1,
        grid=(b, nh, s // tq),
        in_specs=[
            pl.BlockSpec((1, tq, LANES), lambda bi, h, i, lam_r: (bi, i, col0 + h)),
            pl.BlockSpec((1, s, LANES), lambda bi, h, i, lam_r: (bi, 0, col0 + nh + h)),
            pl.BlockSpec((1, s, LANES), lambda bi, h, i, lam_r: (bi, 0, col0 + 2 * nh + h)),
            pl.BlockSpec((1, LANES), lambda bi, h, i, lam_r: (0, 0)),
        ],
        out_specs=pl.BlockSpec((1, tq, LANES), lambda bi, h, i, lam_r: (bi, i, h)),
    )
    return pl.pallas_call(
        functools.partial(_diff_kernel, tk=tk, lambda_init=lambda_init),
        grid_spec=grid_spec,
        out_shape=jax.ShapeDtypeStruct((b, s, nh * LANES), BF16),
        compiler_params=_cparams(("arbitrary", "arbitrary", "arbitrary")),
        name="diff_attn",
    )(lam, proj, proj, proj, subln_g.reshape(1, LANES).astype(F32))


def _dil_kernel(q_ref, k_ref, v_ref, o_ref, lse_ref, *, n_side):
    masks = _head_masks()
    length = q_ref.shape[2]
    tq = DIL_QBLOCK
    tk = tq + 2 * n_side
    rel0 = (lax.broadcasted_iota(jnp.int32, (tq, tk), 1)
            - lax.broadcasted_iota(jnp.int32, (tq, tk), 0))

    def block(i):
        a0 = pl.multiple_of(i * tq, tq)
        ks = pl.multiple_of(jnp.clip(a0 - n_side, 0, length - tk), n_side)
        q = q_ref[0, 0, pl.ds(a0, tq), :]
        k = k_ref[0, 0, pl.ds(ks, tk), :]
        v = v_ref[0, 0, pl.ds(ks, tk), :]
        rel = rel0 + (ks - a0)
        band = (rel >= -n_side) & (rel <= n_side)
        outs, lses = [], []
        for h in range(2):
            qh = jnp.where(masks[h], q, jnp.zeros_like(q))
            sc = jnp.where(band, _dot_nt(qh, k), -jnp.inf)
            m = jnp.max(sc, axis=-1, keepdims=True)
            e = jnp.exp(sc - m)
            den = jnp.sum(e, axis=-1, keepdims=True)
            outs.append(_dot(e.astype(BF16), v) / den)
            lses.append(m + jnp.log(den))
        o_ref[0, 0, pl.ds(a0, tq), :] = jnp.where(masks[0], outs[0], outs[1]).astype(o_ref.dtype)
        lse_ref[0, 0, 0, pl.ds(a0, tq), :] = _columns(lses)

    def body(i, carry):
        block(2 * i)
        block(2 * i + 1)
        return carry

    lax.fori_loop(0, length // (2 * tq), body, 0)


def _dil_attention(proj, window, dil):
    b, _, length, _ = proj.shape
    n_side = (window // 2) // dil
    npair = DIL_HEADS // 2
    cw = DIL_HEADS * HEAD_DIM
    blk = lambda off: pl.BlockSpec((1, 1, length, LANES), lambda bi, c, r: (bi, r, 0, off + c))
    return pl.pallas_call(
        functools.partial(_dil_kernel, n_side=n_side),
        grid=(b, npair, dil),
        in_specs=[blk(0), blk(npair), blk(2 * npair)],
        out_specs=[
            pl.BlockSpec((1, 1, length, LANES), lambda bi, c, r: (bi, r, 0, c)),
            pl.BlockSpec((1, 1, 1, length, 2), lambda bi, c, r: (c, bi, r, 0, 0)),
        ],
        out_shape=[
            jax.ShapeDtypeStruct((b, dil, length, cw), BF16),
            jax.ShapeDtypeStruct((npair, b, dil, length, 2), F32),
        ],
        compiler_params=_cparams(("arbitrary", "arbitrary", "arbitrary")),
        name="dil_attn",
    )(proj, proj, proj)


def _dil_mix_kernel(*refs, dils):
    ng = len(dils)
    o_refs, l_refs = refs[:ng], refs[ng:2 * ng]
    out_ref, so_ref, sl_ref = refs[2 * ng:]
    masks = _head_masks()
    tt = out_ref.shape[1]

    def widen(lse):
        return jnp.where(masks[0], lse[:, 0:1], lse[:, 1:2])

    outs, lses = [], []
    for g, dil in enumerate(dils):
        if dil == 1:
            outs.append(o_refs[g][0, 0].astype(F32))
            lses.append(widen(l_refs[g][0, 0, 0]))
        else:
            n = tt // dil
            for r in range(dil):
                so_ref[g, pl.ds(r, n, stride=dil), :] = o_refs[g][0, r].astype(F32)
                sl_ref[g, pl.ds(r, n, stride=dil), :] = widen(l_refs[g][0, 0, r])
            outs.append(so_ref[g])
            lses.append(sl_ref[g])
    mx = lses[0]
    for l in lses[1:]:
        mx = jnp.maximum(mx, l)
    ws = [jnp.exp(l - mx) for l in lses]
    tot = ws[0]
    for w in ws[1:]:
        tot = tot + w
    acc = None
    for w, o in zip(ws, outs):
        term = (w / tot) * o
        acc = term if acc is None else acc + term
    out_ref[0] = acc.astype(out_ref.dtype)


def _dil_mix(outs, lses, dils, tt=1024):
    b, _, _, cw = outs[0].shape
    s = outs[0].shape[1] * outs[0].shape[2]
    npair = cw // LANES
    ng = len(dils)
    o_specs = [pl.BlockSpec((1, d, tt // d, LANES), lambda bi, i, c: (bi, 0, i, c)) for d in dils]
    l_specs = [pl.BlockSpec((1, 1, d, tt // d, 2), lambda bi, i, c: (c, bi, 0, i, 0)) for d in dils]
    return pl.pallas_call(
        functools.partial(_dil_mix_kernel, dils=tuple(dils)),
        grid=(b, s // tt, npair),
        in_specs=o_specs + l_specs,
        out_specs=pl.BlockSpec((1, tt, LANES), lambda bi, i, c: (bi, i, c)),
        out_shape=jax.ShapeDtypeStruct((b, s, cw), BF16),
        scratch_shapes=[pltpu.VMEM((ng, tt, LANES), F32), pltpu.VMEM((ng, tt, LANES), F32)],
        compiler_params=_cparams(("arbitrary", "arbitrary", "arbitrary")),
        name="dil_mix",
    )(*outs, *lses)


def _norm_router_kernel(x_ref, g_ref, scale_ref, shift_ref, rw_ref, rb_ref, h_ref, idx_ref, gate_ref):
    h = _rms(x_ref[0], g_ref[...]) * (1.0 + scale_ref[0]) + shift_ref[0]
    h_ref[0] = h
    logits = _dot3(h, rw_ref[...]) + rb_ref[...]
    n_e = logits.shape[1]
    eidx = lax.broadcasted_iota(jnp.int32, logits.shape, 1).astype(F32)
    vals, idxs = [], []
    for _ in range(TOP_K):
        m = jnp.max(logits, axis=-1, keepdims=True)
        first = jnp.min(jnp.where(logits == m, eidx, float(n_e)), axis=-1, keepdims=True)
        vals.append(m)
        idxs.append(first)
        logits = jnp.where(eidx == first, -jnp.inf, logits)
    top = _columns(vals)
    e = jnp.exp(top - vals[0])
    gate_ref[0] = e / jnp.sum(e, axis=-1, keepdims=True)
    idx_ref[0] = _columns(idxs).astype(jnp.int32)


def _norm_router(x, g, scale, shift, rw, rb, tm=512):
    b, s, d = x.shape
    n_e = rw.shape[1]
    tok = lambda w: pl.BlockSpec((1, tm, w), lambda bi, i: (bi, i, 0))
    return pl.pallas_call(
        _norm_router_kernel,
        grid=(b, s // tm),
        in_specs=[
            tok(d),
            pl.BlockSpec((1, d), lambda bi, i: (0, 0)),
            pl.BlockSpec((1, 1, d), lambda bi, i: (bi, 0, 0)),
            pl.BlockSpec((1, 1, d), lambda bi, i: (bi, 0, 0)),
            pl.BlockSpec((d, n_e), lambda bi, i: (0, 0)),
            pl.BlockSpec((1, n_e), lambda bi, i: (0, 0)),
        ],
        out_specs=[tok(d), tok(TOP_K), tok(TOP_K)],
        out_shape=[
            jax.ShapeDtypeStruct((b, s, d), F32),
            jax.ShapeDtypeStruct((b, s, TOP_K), jnp.int32),
            jax.ShapeDtypeStruct((b, s, TOP_K), F32),
        ],
        compiler_params=_cparams(("arbitrary", "arbitrary")),
        name="norm_router",
    )(x, g.reshape(1, d), scale, shift, rw, rb.reshape(1, n_e))


def _moe_plan(top_idx, n_tok):
    n_assign = n_tok * TOP_K
    flat_e = top_idx.reshape(-1)
    onehot = (flat_e[:, None] == jnp.arange(N_EXPERTS, dtype=jnp.int32)[None, :]).astype(jnp.int32)
    csum = jnp.cumsum(onehot, axis=0)
    counts = csum[-1]
    rank = jnp.take_along_axis(csum, flat_e[:, None], axis=1)[:, 0] - 1
    padded = (counts + MOE_SUB - 1) // MOE_SUB * MOE_SUB
    pstarts = jnp.cumsum(padded) - padded
    dest = pstarts[flat_e] + rank
    n_slots = n_assign + N_EXPERTS * MOE_SUB
    slot_tok = jnp.zeros((n_slots,), jnp.int32).at[dest].set(
        jnp.arange(n_assign, dtype=jnp.int32) // TOP_K)
    n_items = (n_slots + N_EXPERTS * (MOE_ITEM_ROWS - MOE_SUB)) // MOE_ITEM_ROWS
    items_per_e = (padded + MOE_ITEM_ROWS - 1) // MOE_ITEM_ROWS
    item_ends = jnp.cumsum(items_per_e)
    total_items = item_ends[-1]
    i = jnp.arange(n_items, dtype=jnp.int32)
    i_eff = jnp.minimum(i, total_items - 1)
    e_i = jnp.clip(jnp.searchsorted(item_ends, i_eff, side='right'), 0, N_EXPERTS - 1).astype(jnp.int32)
    local = i_eff - (item_ends[e_i] - items_per_e[e_i])
    row0 = pstarts[e_i] + local * MOE_ITEM_ROWS
    nrows = jnp.clip(padded[e_i] - local * MOE_ITEM_ROWS, 0, MOE_ITEM_ROWS)
    nsub = jnp.where(i < total_items, nrows // MOE_SUB, 0)
    return (e_i.astype(jnp.int32), row0.astype(jnp.int32), nsub.astype(jnp.int32),
            slot_tok, dest.astype(jnp.int32))


def _swiglu_pair(h_a, h_b):
    n = h_a.shape[1]
    lane = lax.broadcasted_iota(jnp.int32, (1, n), 1)
    even = (lane % 2) == 0

    def glu_lin(h):
        glu = jnp.minimum(h, SWIGLU_LIMIT)
        glu = glu * jax.nn.sigmoid(SWIGLU_ALPHA * glu)
        lin = jnp.clip(h, -SWIGLU_LIMIT, SWIGLU_LIMIT) + 1.0
        return glu, lin

    glu_a, lin_a = glu_lin(h_a)
    glu_b, lin_b = glu_lin(h_b)
    act_a = glu_a * pltpu.roll(lin_a, n - 1, 1)
    act_b = pltpu.roll(glu_b, 1, 1) * lin_b
    return jnp.where(even, act_a, act_b)


def _dispatch_kernel(stok_ref, h_hbm, xs_hbm, sem):
    n_slots = xs_hbm.shape[0]
    n_chunks = n_slots // DISPATCH_CHUNK

    def issue(c):
        def body(u, carry):
            base = c * DISPATCH_CHUNK + u * DMA_UNROLL
            for k in range(DMA_UNROLL):
                p = base + k
                pltpu.make_async_copy(h_hbm.at[stok_ref[p]], xs_hbm.at[p], sem).start()
            return carry

        lax.fori_loop(0, DISPATCH_CHUNK // DMA_UNROLL, body, 0)

    def wait(c):
        pltpu.make_async_copy(h_hbm.at[pl.ds(0, DISPATCH_CHUNK)],
                              xs_hbm.at[pl.ds(c * DISPATCH_CHUNK, DISPATCH_CHUNK)], sem).wait()

    issue(0)

    def step(c, carry):
        issue(c)
        wait(c - 1)
        return carry

    lax.fori_loop(1, n_chunks, step, 0)
    wait(n_chunks - 1)


def _moe_dispatch(h, slot_tok):
    n_slots = slot_tok.shape[0]
    assert n_slots % DISPATCH_CHUNK == 0
    grid_spec = pltpu.PrefetchScalarGridSpec(
        num_scalar_prefetch=1,
        grid=(1,),
        in_specs=[pl.BlockSpec(memory_space=pl.ANY)],
        out_specs=pl.BlockSpec(memory_space=pl.ANY),
        scratch_shapes=[pltpu.SemaphoreType.DMA(())],
    )
    return pl.pallas_call(
        _dispatch_kernel,
        grid_spec=grid_spec,
        out_shape=jax.ShapeDtypeStruct((n_slots, h.shape[1]), h.dtype),
        compiler_params=_cparams(("arbitrary",)),
        name="moe_dispatch",
    )(slot_tok, h)


def _for_each_sub(nsub, fn):
    def pair(p, carry):
        fn(2 * p)
        fn(2 * p + 1)
        return carry

    lax.fori_loop(0, nsub // 2, pair, 0)

    @pl.when(nsub % 2 == 1)
    def _():
        fn(nsub - 1)


def _moe_kernel(ie_ref, row0_ref, nsub_ref,
                xs_hbm, wgu_a_ref, wgu_b_ref, bgu_a_ref, bgu_b_ref, wdn_ref, bdn_ref,
                y_hbm,
                xs_ref, act_ref, ot_ref, wgu_bf_ref, il_ref, wdn_bf_ref, xsem, osem, *, n_gu):
    i = pl.program_id(0)
    s = pl.program_id(1)
    n_items = pl.num_programs(0)
    n_steps = pl.num_programs(1)
    nsub = nsub_ref[i]
    row0 = row0_ref[i]
    slot = i % 2
    ft2 = 2 * MOE_FT

    def sub_rows(sb):
        return pl.ds(pl.multiple_of(sb * MOE_SUB, MOE_SUB), MOE_SUB)

    def xs_copy(item, sb):
        src0 = pl.multiple_of(row0_ref[item] + sb * MOE_SUB, MOE_SUB)
        return pltpu.make_async_copy(xs_hbm.at[pl.ds(src0, MOE_SUB)],
                                     xs_ref.at[item % 2, pl.ds(sb * MOE_SUB, MOE_SUB)],
                                     xsem.at[item % 2])

    def for_item_subs(item, fn):
        for sb in range(MOE_NSUB):
            @pl.when(sb < nsub_ref[item])
            def _():
                fn(sb)

    @pl.when((s == 0) & (nsub > 0))
    def _fetch_rows():
        @pl.when(i == 0)
        def _():
            for_item_subs(i, lambda sb: xs_copy(i, sb).start())

        for_item_subs(i, lambda sb: xs_copy(i, sb).wait())

        @pl.when(i + 1 < n_items)
        def _():
            for_item_subs(i + 1, lambda sb: xs_copy(i + 1, sb).start())

    @pl.when((s < n_gu) & (nsub > 0))
    def _gate_up():
        wgu_bf_ref[:, 0:ft2] = wgu_a_ref[0, 0].astype(BF16)
        wgu_bf_ref[:, ft2:2 * ft2] = wgu_b_ref[0, 0].astype(BF16)
        b_a = bgu_a_ref[0, 0]
        b_b = bgu_b_ref[0, 0]

        def one(sb):
            rows = sub_rows(sb)
            x = xs_ref[slot, rows, :].astype(BF16)
            hgu = _dot(x, wgu_bf_ref[...])
            act = _swiglu_pair(hgu[:, 0:ft2] + b_a, hgu[:, ft2:2 * ft2] + b_b)
            act_ref[s, rows, :] = act.astype(BF16)

        _for_each_sub(nsub, one)

    def out_copy(osl, n, sb):
        dst0 = pl.multiple_of(row0 + sb * MOE_SUB, MOE_SUB)
        return pltpu.make_async_copy(
            ot_ref.at[osl, pl.ds(sb * MOE_SUB, MOE_SUB)],
            y_hbm.at[pl.ds(dst0, MOE_SUB), pl.ds(pl.multiple_of(n * MOE_TILE, MOE_TILE), MOE_TILE)],
            osem.at[osl])

    @pl.when((s >= n_gu) & (nsub > 0))
    def _down():
        n = s - n_gu
        osl = n % 2

        @pl.when(n >= 2)
        def _():
            for_item_subs(i, lambda sb: out_copy(osl, 0, sb).wait())

        half = wdn_ref.shape[2] // 2
        for c in range(MOE_TILE // LANES):
            cols = slice(c * LANES, (c + 1) * LANES)
            il_ref[c, pl.ds(0, half, stride=2), :] = wdn_ref[0, 0, 0:half, cols]
            il_ref[c, pl.ds(1, half, stride=2), :] = wdn_ref[0, 0, half:2 * half, cols]
            wdn_bf_ref[:, cols] = il_ref[c].astype(BF16)
        bias = bdn_ref[0, 0]

        def one(sb):
            rows = sub_rows(sb)
            acc = None
            for f in range(n_gu):
                term = _dot(act_ref[f, rows, :], wdn_bf_ref[f * MOE_TILE:(f + 1) * MOE_TILE, :])
                acc = term if acc is None else acc + term
            ot_ref[osl, rows, :] = acc + bias

        _for_each_sub(nsub, one)
        for_item_subs(i, lambda sb: out_copy(osl, n, sb).start())

        @pl.when(s == n_steps - 1)
        def _():
            for_item_subs(i, lambda sb: out_copy(1 - osl, 0, sb).wait())
            for_item_subs(i, lambda sb: out_copy(osl, 0, sb).wait())

def _moe_experts(xs, plan, w_gu, b_gu, w_dn, b_dn, layer):
    n_slots, d = xs.shape
    _, n_e, _, two_f = w_gu.shape
    d_ff = two_f // 2
    item_e, item_row0, item_nsub = plan
    n_items = item_e.shape[0]
    n_gu = (d_ff // 2) // MOE_FT
    n_dn = d // MOE_TILE
    ft2 = 2 * MOE_FT
    assert w_dn.shape[2] == n_gu * MOE_TILE

    def gu_tile(i, s, ns):
        return jnp.where(ns[i] > 0, jnp.minimum(s, n_gu - 1), n_gu - 1)

    def dn_tile(i, s, ns):
        return jnp.where(ns[i] > 0, jnp.maximum(s - n_gu, 0), n_dn - 1)

    def gu_map(half):
        return lambda i, s, ie, r0, ns: (layer, ie[i], 0, half * n_gu + gu_tile(i, s, ns))

    dn_map = lambda i, s, ie, r0, ns: (layer, ie[i], 0, dn_tile(i, s, ns))

    grid_spec = pltpu.PrefetchScalarGridSpec(
        num_scalar_prefetch=3,
        grid=(n_items, n_gu + n_dn),
        in_specs=[
            pl.BlockSpec(memory_space=pl.ANY),
            pl.BlockSpec((1, 1, d, ft2), gu_map(0)),
            pl.BlockSpec((1, 1, d, ft2), gu_map(1)),
            pl.BlockSpec((1, 1, 1, ft2), gu_map(0)),
            pl.BlockSpec((1, 1, 1, ft2), gu_map(1)),
            pl.BlockSpec((1, 1, d_ff, MOE_TILE), dn_map),
            pl.BlockSpec((1, 1, 1, MOE_TILE), dn_map),
        ],
        out_specs=pl.BlockSpec(memory_space=pl.ANY),
        scratch_shapes=[
            pltpu.VMEM((2, MOE_ITEM_ROWS, d), F32),
            pltpu.VMEM((n_gu, MOE_ITEM_ROWS, MOE_TILE), BF16),
            pltpu.VMEM((2, MOE_ITEM_ROWS, MOE_TILE), F32),
            pltpu.VMEM((d, 2 * ft2), BF16),
            pltpu.VMEM((MOE_TILE // LANES, d_ff, LANES), F32),
            pltpu.VMEM((d_ff, MOE_TILE), BF16),
            pltpu.SemaphoreType.DMA((2,)),
            pltpu.SemaphoreType.DMA((2,)),
        ],
    )
    depth = w_gu.shape[0]
    return pl.pallas_call(
        functools.partial(_moe_kernel, n_gu=n_gu),
        grid_spec=grid_spec,
        out_shape=jax.ShapeDtypeStruct((n_slots, d), F32),
        input_output_aliases={3: 0},
        compiler_params=_cparams(("arbitrary", "arbitrary")),
        name="moe_experts",
    )(item_e, item_row0, item_nsub, xs, w_gu, w_gu,
      b_gu.reshape(depth, n_e, 1, two_f), b_gu.reshape(depth, n_e, 1, two_f),
      w_dn, b_dn.reshape(depth, n_e, 1, d))


def _moe_combine_kernel(pos_ref, x_ref, g_ref, gate_ref, y_hbm, o_ref, ybuf_ref, sem):
    i = pl.program_id(0)
    n = pl.num_programs(0)
    tt = x_ref.shape[0]

    def issue(step):
        sl = step % 2

        def body(u, carry):
            for q in range(DMA_UNROLL // TOP_K):
                t = u * (DMA_UNROLL // TOP_K) + q
                for k in range(TOP_K):
                    p = pos_ref[(step * tt + t) * TOP_K + k]
                    pltpu.make_async_copy(y_hbm.at[p], ybuf_ref.at[sl, k, t], sem.at[sl]).start()
            return carry

        lax.fori_loop(0, tt * TOP_K // DMA_UNROLL, body, 0)

    @pl.when(i == 0)
    def _():
        issue(i)

    @pl.when(i + 1 < n)
    def _():
        issue(i + 1)

    slot = i % 2
    pltpu.make_async_copy(ybuf_ref.at[slot], ybuf_ref.at[slot], sem.at[slot]).wait()
    gates = g_ref[...]
    tot = None
    for k in range(TOP_K):
        term = gates[:, k:k + 1] * ybuf_ref[slot, k]
        tot = term if tot is None else tot + term
    o_ref[...] = x_ref[...] + gate_ref[0] * tot


def _moe_combine(x, y, pos, gates, gate_mod):
    b, s, d = x.shape
    tt = COMBINE_TOKENS
    n_tok = b * s
    grid_spec = pltpu.PrefetchScalarGridSpec(
        num_scalar_prefetch=1,
        grid=(n_tok // tt,),
        in_specs=[
            pl.BlockSpec((tt, d), lambda i, pos_r: (i, 0)),
            pl.BlockSpec((tt, TOP_K), lambda i, pos_r: (i, 0)),
            pl.BlockSpec((1, 1, d), lambda i, pos_r: (i * tt // s, 0, 0)),
            pl.BlockSpec(memory_space=pl.ANY),
        ],
        out_specs=pl.BlockSpec((tt, d), lambda i, pos_r: (i, 0)),
        scratch_shapes=[pltpu.VMEM((2, TOP_K, tt, d), F32), pltpu.SemaphoreType.DMA((2,))],
    )
    out = pl.pallas_call(
        _moe_combine_kernel,
        grid_spec=grid_spec,
        out_shape=jax.ShapeDtypeStruct((n_tok, d), F32),
        compiler_params=_cparams(("arbitrary",)),
        name="moe_combine",
    )(pos, x.reshape(n_tok, d), gates.reshape(n_tok, TOP_K), gate_mod, y)
    return out.reshape(b, s, d)


def _moe_ffn(x, g, scale, shift, gate_mod, rw, rb, w_gu, b_gu, w_dn, b_dn, layer):
    b, s, d = x.shape
    h, top_idx, gates = _norm_router(x, g, scale, shift, rw, rb)
    item_e, item_row0, item_nsub, slot_tok, pos = _moe_plan(top_idx.reshape(b * s, TOP_K), b * s)
    xs = _moe_dispatch(h.reshape(b * s, d), slot_tok)
    y = _moe_experts(xs, (item_e, item_row0, item_nsub), w_gu, b_gu, w_dn, b_dn, layer)
    return _moe_combine(x, y, pos, gates, gate_mod)


def _rope_lane_tables(seq):
    inv_freq = ROPE_THETA ** (-jnp.arange(0, HEAD_DIM, 2, dtype=F32) / HEAD_DIM)
    ang = jnp.arange(seq, dtype=F32)[:, None] * inv_freq[None, :]
    cos, sin = jnp.cos(ang), jnp.sin(ang)
    reps = LANES // HEAD_DIM
    return (jnp.tile(jnp.concatenate([cos, cos], axis=-1), (1, reps)),
            jnp.tile(jnp.concatenate([-sin, sin], axis=-1), (1, reps)))


def _col_modes(kinds, tn, width):
    modes = []
    for n_cols, mode in kinds:
        assert n_cols % tn == 0
        modes += [mode] * (n_cols // tn)
    assert len(modes) * tn == width
    return jnp.asarray(modes, jnp.int32)


PLAIN, ROPE, SCALE, ROPE_SCALE = 0, 1, 2, 3


def kernel(x, c, ada_w, ada_b, norm_g, final_g, ab_w_in, ab_w_out, na_rpb, diff_lam_q1, diff_lam_k1,
           diff_lam_q2, diff_lam_k2, diff_subln_g, dil_w_in, dil_w_out, moe_router_w, moe_router_b,
           moe_w_gate_up, moe_b_gate_up, moe_w_down, moe_b_down):
    b, s, d = x.shape
    depth = ada_w.shape[0]
    tn = 512
    cos_t, sin_t = _rope_lane_tables(s)

    c_pad = jnp.zeros((8, d), F32).at[:b].set(c)
    mod = _adaln(c_pad, ada_w.reshape(depth * 2, d, 3 * d), ada_b.reshape(depth * 2, 3 * d))
    mod = mod[:, :b].reshape(depth, 2, b, 3, 1, d)

    na_w = NA_HEADS * HEAD_DIM
    df_w = DIFF_HEADS * 2 * HEAD_DIM
    dil_w = DIL_HEADS * HEAD_DIM

    for l in range(depth):
        shift, scale, gate = (mod[l, 0, :, t] for t in range(3))
        h = _norm_mod(x, norm_g[l, 0], scale, shift, BF16)
        i = l // 2
        if l % 2 == 0:
            lambda_init = 0.8 - 0.6 * math.exp(-0.3 * l)
            modes = _col_modes([(na_w, SCALE), (2 * na_w, PLAIN), (df_w, ROPE_SCALE), (df_w, ROPE),
                                (df_w, PLAIN)], tn, ab_w_in.shape[2])
            proj = _proj(h, ab_w_in[i], modes, cos_t, sin_t, ab_w_in.shape[2], tn=tn)
            proj = proj.reshape(b, s, ab_w_in.shape[2])
            o_na = _na_attention(proj, _na_bias_table(na_rpb[i], s // GRID_W))
            o_df = _diff_attention(proj, diff_lam_q1[i], diff_lam_k1[i], diff_lam_q2[i], diff_lam_k2[i],
                                   diff_subln_g[i], lambda_init, col0=3 * na_w // LANES)
            o = jnp.concatenate([o_na, o_df], axis=-1)
            x = _outproj(o, ab_w_out[i], x, gate)
        else:
            modes = _col_modes([(dil_w, ROPE_SCALE), (dil_w, ROPE), (dil_w, PLAIN)] * len(DIL_CONFIGS),
                               tn, dil_w_in.shape[2])
            outs, lses = [], []
            for gi, (window, dil) in enumerate(DIL_CONFIGS):
                proj = _proj(h, dil_w_in[i], modes, cos_t, sin_t, 3 * dil_w, jbase=gi * (3 * dil_w // tn),
                             dil=dil, tn=tn)
                o_g, lse_g = _dil_attention(proj, window, dil)
                outs.append(o_g)
                lses.append(lse_g)
            o = _dil_mix(outs, lses, [dil for _, dil in DIL_CONFIGS])
            x = _outproj(o, dil_w_out[i], x, gate)
        shift, scale, gate = (mod[l, 1, :, t] for t in range(3))
        x = _moe_ffn(x, norm_g[l, 1], scale, shift, gate, moe_router_w[l], moe_router_b[l],
                     moe_w_gate_up, moe_b_gate_up, moe_w_down, moe_b_down, l)
    return _final_norm(x, final_g)
```

```python
import functools
import math

import jax
import jax.numpy as jnp
from jax import lax
from jax.experimental import pallas as pl
from jax.experimental.pallas import tpu as pltpu

F32 = jnp.float32
BF16 = jnp.bfloat16

HEAD_DIM = 64
LANES = 128
GRID_W = 64
NA_HEADS = 16
NA_ROWS = 8
NA_COLS = 16
DIFF_HEADS = 8
DIL_HEADS = 32
DIL_CONFIGS = ((128, 1), (512, 4), (2048, 16))
ROPE_THETA = 10000.0
N_EXPERTS = 32
TOP_K = 4
SWIGLU_LIMIT = 7.0
SWIGLU_ALPHA = 1.702
RMS_EPS = 1e-5

DIL_QBLOCK = 128
MOE_SUB = 256
MOE_NSUB = 5
MOE_ITEM_ROWS = MOE_NSUB * MOE_SUB
MOE_FT = 128
MOE_TILE = 2 * MOE_FT
DISPATCH_ROWS = 256
DMA_UNROLL = 8
COMBINE_TOKENS = 128
VMEM_LIMIT = 56 * 1024 * 1024


def _cparams(sem, vmem=VMEM_LIMIT):
    return pltpu.CompilerParams(dimension_semantics=sem, vmem_limit_bytes=vmem)


def _dot(a, b):
    return jnp.dot(a, b, preferred_element_type=F32)


def _dot_nt(a, b):
    return lax.dot_general(a, b, (((1,), (1,)), ((), ())), preferred_element_type=F32)


def _split_bf16(a):
    hi = a.astype(BF16)
    lo = (a - hi.astype(F32)).astype(BF16)
    return hi, lo


def _dot3(a, b):
    ah, al = _split_bf16(a)
    bh, bl = _split_bf16(b)
    return _dot(ah, bh) + (_dot(ah, bl) + _dot(al, bh))


def _adaln_kernel(c_ref, w_ref, b_ref, o_ref):
    c = c_ref[...]
    c_act = c * jax.nn.sigmoid(c)
    o_ref[0] = _dot3(c_act, w_ref[0]) + b_ref[0]


def _adaln(c_pad, w, b, tn=768):
    n_mod, d, n = w.shape
    return pl.pallas_call(
        _adaln_kernel,
        grid=(n_mod, n // tn),
        in_specs=[
            pl.BlockSpec(c_pad.shape, lambda l, j: (0, 0)),
            pl.BlockSpec((1, d, tn), lambda l, j: (l, 0, j)),
            pl.BlockSpec((1, 1, tn), lambda l, j: (l, 0, j)),
        ],
        out_specs=pl.BlockSpec((1, c_pad.shape[0], tn), lambda l, j: (l, 0, j)),
        out_shape=jax.ShapeDtypeStruct((n_mod, c_pad.shape[0], n), F32),
        compiler_params=_cparams(("arbitrary", "arbitrary")),
        name="adaln",
    )(c_pad, w, b.reshape(n_mod, 1, n))


def _rms(x, g):
    return x * lax.rsqrt(jnp.mean(x * x, axis=-1, keepdims=True) + RMS_EPS) * g


def _norm_mod_kernel(x_ref, g_ref, scale_ref, shift_ref, o_ref):
    h = _rms(x_ref[0], g_ref[...]) * (1.0 + scale_ref[0]) + shift_ref[0]
    o_ref[0] = h.astype(o_ref.dtype)


def _norm_mod(x, g, scale, shift, out_dtype, tm=512):
    b, s, d = x.shape
    return pl.pallas_call(
        _norm_mod_kernel,
        grid=(b, s // tm),
        in_specs=[
            pl.BlockSpec((1, tm, d), lambda bi, i: (bi, i, 0)),
            pl.BlockSpec((1, d), lambda bi, i: (0, 0)),
            pl.BlockSpec((1, 1, d), lambda bi, i: (bi, 0, 0)),
            pl.BlockSpec((1, 1, d), lambda bi, i: (bi, 0, 0)),
        ],
        out_specs=pl.BlockSpec((1, tm, d), lambda bi, i: (bi, i, 0)),
        out_shape=jax.ShapeDtypeStruct((b, s, d), out_dtype),
        compiler_params=_cparams(("arbitrary", "arbitrary")),
        name="norm_mod",
    )(x, g.reshape(1, d), scale, shift)


def _final_norm_kernel(x_ref, g_ref, o_ref):
    o_ref[0] = _rms(x_ref[0], g_ref[...])


def _final_norm(x, g, tm=512):
    b, s, d = x.shape
    return pl.pallas_call(
        _final_norm_kernel,
        grid=(b, s // tm),
        in_specs=[
            pl.BlockSpec((1, tm, d), lambda bi, i: (bi, i, 0)),
            pl.BlockSpec((1, d), lambda bi, i: (0, 0)),
        ],
        out_specs=pl.BlockSpec((1, tm, d), lambda bi, i: (bi, i, 0)),
        out_shape=jax.ShapeDtypeStruct((b, s, d), F32),
        compiler_params=_cparams(("arbitrary", "arbitrary")),
        name="final_norm",
    )(x, g.reshape(1, d))


def _proj_kernel(mode_ref, a_ref, w_ref, cos_ref, sin_ref, o_ref, wbf_ref, stage_ref, *, jbase, dil):
    j = pl.program_id(0)

    @pl.when((pl.program_id(1) == 0) & (pl.program_id(2) == 0))
    def _():
        wbf_ref[...] = w_ref[...].astype(BF16)

    acc = _dot(a_ref[0], wbf_ref[...])
    mode = mode_ref[jbase + j]
    tm, tn = acc.shape

    def emit(ci, val):
        cols = slice(ci * LANES, (ci + 1) * LANES)
        if dil == 1:
            o_ref[0, 0, :, cols] = val.astype(o_ref.dtype)
        else:
            stage_ref[ci] = val
            for r in range(dil):
                o_ref[0, r, :, cols] = stage_ref[ci, pl.ds(r, tm // dil, stride=dil), :].astype(o_ref.dtype)

    @pl.when(mode == 0)
    def _():
        for ci in range(tn // LANES):
            emit(ci, acc[:, ci * LANES:(ci + 1) * LANES])

    @pl.when(mode != 0)
    def _():
        rope_on = (mode & 1).astype(F32)
        qscale = jnp.where((mode & 2) == 2, HEAD_DIM ** -0.5, 1.0).astype(F32)
        lane = lax.broadcasted_iota(jnp.int32, (1, LANES), 1)
        first_half = (lane % HEAD_DIM) < (HEAD_DIM // 2)
        cos = (cos_ref[...] * rope_on + (1.0 - rope_on)) * qscale
        sin = sin_ref[...] * (rope_on * qscale)
        for ci in range(tn // LANES):
            x = acc[:, ci * LANES:(ci + 1) * LANES]
            partner = jnp.where(first_half,
                                pltpu.roll(x, LANES - HEAD_DIM // 2, 1),
                                pltpu.roll(x, HEAD_DIM // 2, 1))
            emit(ci, x * cos + partner * sin)


def _proj(a, w, modes, cos_t, sin_t, n_out, jbase=0, dil=1, tm=1024, tn=512):
    b, s, k = a.shape
    grid_spec = pltpu.PrefetchScalarGridSpec(
        num_scalar_prefetch=1,
        grid=(n_out // tn, b, s // tm),
        in_specs=[
            pl.BlockSpec((1, tm, k), lambda j, bi, i, m: (bi, i, 0)),
            pl.BlockSpec((k, tn), lambda j, bi, i, m: (0, jbase + j)),
            pl.BlockSpec((tm, LANES), lambda j, bi, i, m: (i, 0)),
            pl.BlockSpec((tm, LANES), lambda j, bi, i, m: (i, 0)),
        ],
        out_specs=pl.BlockSpec((1, dil, tm // dil, tn), lambda j, bi, i, m: (bi, 0, i, j)),
        scratch_shapes=[pltpu.VMEM((k, tn), BF16), pltpu.VMEM((tn // LANES, tm, LANES), F32)],
    )
    return pl.pallas_call(
        functools.partial(_proj_kernel, jbase=jbase, dil=dil),
        grid_spec=grid_spec,
        out_shape=jax.ShapeDtypeStruct((b, dil, s // dil, n_out), BF16),
        compiler_params=_cparams(("arbitrary", "arbitrary", "arbitrary")),
        name="proj",
    )(modes, a, w, cos_t, sin_t)


def _outproj_kernel(a_ref, w_ref, x_ref, gate_ref, o_ref, wbf_ref):
    @pl.when((pl.program_id(1) == 0) & (pl.program_id(2) == 0))
    def _():
        wbf_ref[...] = w_ref[...].astype(BF16)

    o_ref[0] = x_ref[0] + gate_ref[0] * _dot(a_ref[0], wbf_ref[...])


def _outproj(a, w, x, gate, tm=1024, tn=512):
    b, s, k = a.shape
    n = w.shape[1]
    return pl.pallas_call(
        _outproj_kernel,
        grid=(n // tn, b, s // tm),
        in_specs=[
            pl.BlockSpec((1, tm, k), lambda j, bi, i: (bi, i, 0)),
            pl.BlockSpec((k, tn), lambda j, bi, i: (0, j)),
            pl.BlockSpec((1, tm, tn), lambda j, bi, i: (bi, i, j)),
            pl.BlockSpec((1, 1, tn), lambda j, bi, i: (bi, 0, j)),
        ],
        out_specs=pl.BlockSpec((1, tm, tn), lambda j, bi, i: (bi, i, j)),
        out_shape=jax.ShapeDtypeStruct((b, s, n), F32),
        scratch_shapes=[pltpu.VMEM((k, tn), BF16)],
        compiler_params=_cparams(("arbitrary", "arbitrary", "arbitrary")),
        name="outproj",
    )(a, w, x, gate)


def _columns(cols):
    m = cols[0].shape[0]
    lane = lax.broadcasted_iota(jnp.int32, (m, len(cols)), 1)
    out = jnp.broadcast_to(cols[0], (m, len(cols)))
    for k in range(1, len(cols)):
        out = jnp.where(lane == k, cols[k], out)
    return out


def _head_masks():
    lane = lax.broadcasted_iota(jnp.int32, (1, LANES), 1)
    return lane < HEAD_DIM, lane >= HEAD_DIM


def _na_kernel(q_ref, k_ref, v_ref, bias_ref, o_ref, *, rows):
    kr = min(NA_ROWS, rows)
    masks = _head_masks()

    def body(r, carry):
        rs = jnp.clip(r - kr // 2, 0, rows - kr)
        var = r - rs
        q = q_ref[0, pl.ds(pl.multiple_of(r * GRID_W, GRID_W), GRID_W), :]
        k = k_ref[0, pl.ds(pl.multiple_of(rs * GRID_W, GRID_W), kr * GRID_W), :]
        v = v_ref[0, pl.ds(pl.multiple_of(rs * GRID_W, GRID_W), kr * GRID_W), :]
        outs = []
        for h in range(2):
            qh = jnp.where(masks[h], q, jnp.zeros_like(q))
            sc = _dot_nt(qh, k) + bias_ref[h, var]
            m = jnp.max(sc, axis=-1, keepdims=True)
            e = jnp.exp(sc - m)
            den = jnp.sum(e, axis=-1, keepdims=True)
            outs.append(_dot(e.astype(BF16), v) / den)
        o = jnp.where(masks[0], outs[0], outs[1])
        o_ref[0, pl.ds(pl.multiple_of(r * GRID_W, GRID_W), GRID_W), :] = o.astype(o_ref.dtype)
        return carry

    lax.fori_loop(0, rows, body, 0)


def _na_bias_table(rpb, rows):
    kr = min(NA_ROWS, rows)
    c = jnp.arange(GRID_W)
    kc = jnp.arange(GRID_W)
    cs = jnp.clip(c - NA_COLS // 2, 0, GRID_W - NA_COLS)
    valid = (kc[None, :] >= cs[:, None]) & (kc[None, :] < cs[:, None] + NA_COLS)
    coff = jnp.clip(kc[None, :] - c[:, None] + (NA_COLS - 1), 0, 2 * NA_COLS - 2)
    rows_v = jnp.stack([rpb[:, NA_ROWS - 1 - v:NA_ROWS - 1 - v + kr, :] for v in range(kr)], axis=1)
    onehot = ((coff[None] == jnp.arange(2 * NA_COLS - 1)[:, None, None]) & valid[None]).astype(F32)
    tbl = jnp.einsum('hvio,ock->hvcik', rows_v.astype(F32), onehot, precision=lax.Precision.HIGHEST)
    tbl = jnp.where(valid[None, None, :, None, :], tbl, -jnp.inf)
    return tbl.reshape(rpb.shape[0], kr, GRID_W, kr * GRID_W)


def _na_attention(proj, bias_tbl):
    b, s, _ = proj.shape
    rows = s // GRID_W
    npair = NA_HEADS // 2
    kr = bias_tbl.shape[1]
    blk = lambda off: pl.BlockSpec((1, s, LANES), lambda bi, c: (bi, 0, off + c))
    return pl.pallas_call(
        functools.partial(_na_kernel, rows=rows),
        grid=(b, npair),
        in_specs=[
            blk(0), blk(npair), blk(2 * npair),
            pl.BlockSpec((2, kr, GRID_W, kr * GRID_W), lambda bi, c: (c, 0, 0, 0)),
        ],
        out_specs=pl.BlockSpec((1, s, LANES), lambda bi, c: (bi, 0, c)),
        out_shape=jax.ShapeDtypeStruct((b, s, NA_HEADS * HEAD_DIM), BF16),
        compiler_params=_cparams(("arbitrary", "arbitrary")),
        name="na_attn",
    )(proj, proj, proj, bias_tbl)


def _diff_kernel(lam_ref, q_ref, k_ref, v_ref, g_ref, o_ref, *, tk, lambda_init):
    masks = _head_masks()
    q = q_ref[0]
    tq = q.shape[0]
    s = k_ref.shape[1]
    qs = [jnp.where(masks[m], q, jnp.zeros_like(q)) for m in range(2)]

    def body(j, carry):
        k = k_ref[0, pl.ds(pl.multiple_of(j * tk, tk), tk), :]
        v = v_ref[0, pl.ds(pl.multiple_of(j * tk, tk), tk), :]
        new = []
        for m in range(2):
            m_old, l_old, acc_old = carry[3 * m:3 * m + 3]
            sc = _dot_nt(qs[m], k)
            m_new = jnp.maximum(m_old, jnp.max(sc, axis=-1, keepdims=True))
            alpha = jnp.exp(m_old - m_new)
            e = jnp.exp(sc - m_new)
            l_new = alpha * l_old + jnp.sum(e, axis=-1, keepdims=True)
            acc_new = alpha * acc_old + _dot(e.astype(BF16), v)
            new += [m_new, l_new, acc_new]
        return tuple(new)

    init = []
    for m in range(2):
        init += [jnp.full((tq, 1), -jnp.inf, F32), jnp.zeros((tq, 1), F32), jnp.zeros((tq, LANES), F32)]
    res = lax.fori_loop(0, s // tk, body, tuple(init))
    lam = lam_ref[0]
    o = res[2] / res[1] - lam * (res[5] / res[4])
    o_ref[0] = (_rms(o, g_ref[...]) * (1.0 - lambda_init)).astype(o_ref.dtype)


def _diff_lambda_kernel(lq1_ref, lk1_ref, lq2_ref, lk2_ref, o_ref, *, lambda_init):
    s1 = jnp.sum(lq1_ref[...] * lk1_ref[...], axis=-1, keepdims=True)
    s2 = jnp.sum(lq2_ref[...] * lk2_ref[...], axis=-1, keepdims=True)
    o_ref[...] = jnp.exp(s1) - jnp.exp(s2) + lambda_init


def _diff_attention(proj, lq1, lk1, lq2, lk2, subln_g, lambda_init, col0, tq=512, tk=512):
    b, s, _ = proj.shape
    nh = DIFF_HEADS
    vec = lambda a: a.reshape(1, HEAD_DIM).astype(F32)
    lam = pl.pallas_call(
        functools.partial(_diff_lambda_kernel, lambda_init=lambda_init),
        out_shape=jax.ShapeDtypeStruct((1, 1), F32),
        name="diff_lambda",
    )(vec(lq1), vec(lk1), vec(lq2), vec(lk2)).reshape(1)
    grid_spec = pltpu.PrefetchScalarGridSpec(
        num_scalar_prefetch=1,
        grid=(b, nh, s // tq),
        in_specs=[
            pl.BlockSpec((1, tq, LANES), lambda bi, h, i, lam_r: (bi, i, col0 + h)),
            pl.BlockSpec((1, s, LANES), lambda bi, h, i, lam_r: (bi, 0, col0 + nh + h)),
            pl.BlockSpec((1, s, LANES), lambda bi, h, i, lam_r: (bi, 0, col0 + 2 * nh + h)),
            pl.BlockSpec((1, LANES), lambda bi, h, i, lam_r: (0, 0)),
        ],
        out_specs=pl.BlockSpec((1, tq, LANES), lambda bi, h, i, lam_r: (bi, i, h)),
    )
    return pl.pallas_call(
        functools.partial(_diff_kernel, tk=tk, lambda_init=lambda_init),
        grid_spec=grid_spec,
        out_shape=jax.ShapeDtypeStruct((b, s, nh * LANES), BF16),
        compiler_params=_cparams(("arbitrary", "arbitrary", "arbitrary")),
        name="diff_attn",
    )(lam, proj, proj, proj, subln_g.reshape(1, LANES).astype(F32))


def _dil_kernel(q_ref, k_ref, v_ref, o_ref, lse_ref, *, n_side):
    masks = _head_masks()
    length = q_ref.shape[2]
    tq = DIL_QBLOCK
    tk = tq + 2 * n_side
    rel0 = (lax.broadcasted_iota(jnp.int32, (tq, tk), 1)
            - lax.broadcasted_iota(jnp.int32, (tq, tk), 0))

    def block(i):
        a0 = pl.multiple_of(i * tq, tq)
        ks = pl.multiple_of(jnp.clip(a0 - n_side, 0, length - tk), n_side)
        q = q_ref[0, 0, pl.ds(a0, tq), :]
        k = k_ref[0, 0, pl.ds(ks, tk), :]
        v = v_ref[0, 0, pl.ds(ks, tk), :]
        rel = rel0 + (ks - a0)
        band = (rel >= -n_side) & (rel <= n_side)
        outs, lses = [], []
        for h in range(2):
            qh = jnp.where(masks[h], q, jnp.zeros_like(q))
            sc = jnp.where(band, _dot_nt(qh, k), -jnp.inf)
            m = jnp.max(sc, axis=-1, keepdims=True)
            e = jnp.exp(sc - m)
            den = jnp.sum(e, axis=-1, keepdims=True)
            outs.append(_dot(e.astype(BF16), v) / den)
            lses.append(m + jnp.log(den))
        o_ref[0, 0, pl.ds(a0, tq), :] = jnp.where(masks[0], outs[0], outs[1]).astype(o_ref.dtype)
        lse_ref[0, 0, 0, pl.ds(a0, tq), :] = _columns(lses)

    def body(i, carry):
        block(2 * i)
        block(2 * i + 1)
        return carry

    lax.fori_loop(0, length // (2 * tq), body, 0)


def _dil_attention(proj, window, dil):
    b, _, length, _ = proj.shape
    n_side = (window // 2) // dil
    npair = DIL_HEADS // 2
    cw = DIL_HEADS * HEAD_DIM
    blk = lambda off: pl.BlockSpec((1, 1, length, LANES), lambda bi, c, r: (bi, r, 0, off + c))
    return pl.pallas_call(
        functools.partial(_dil_kernel, n_side=n_side),
        grid=(b, npair, dil),
        in_specs=[blk(0), blk(npair), blk(2 * npair)],
        out_specs=[
            pl.BlockSpec((1, 1, length, LANES), lambda bi, c, r: (bi, r, 0, c)),
            pl.BlockSpec((1, 1, 1, length, 2), lambda bi, c, r: (c, bi, r, 0, 0)),
        ],
        out_shape=[
            jax.ShapeDtypeStruct((b, dil, length, cw), BF16),
            jax.ShapeDtypeStruct((npair, b, dil, length, 2), F32),
        ],
        compiler_params=_cparams(("arbitrary", "arbitrary", "arbitrary")),
        name="dil_attn",
    )(proj, proj, proj)


def _dil_mix_kernel(*refs, dils):
    ng = len(dils)
    o_refs, l_refs = refs[:ng], refs[ng:2 * ng]
    out_ref, so_ref, sl_ref = refs[2 * ng:]
    masks = _head_masks()
    tt = out_ref.shape[1]

    def widen(lse):
        return jnp.where(masks[0], lse[:, 0:1], lse[:, 1:2])

    outs, lses = [], []
    for g, dil in enumerate(dils):
        if dil == 1:
            outs.append(o_refs[g][0, 0].astype(F32))
            lses.append(widen(l_refs[g][0, 0, 0]))
        else:
            n = tt // dil
            for r in range(dil):
                so_ref[g, pl.ds(r, n, stride=dil), :] = o_refs[g][0, r].astype(F32)
                sl_ref[g, pl.ds(r, n, stride=dil), :] = widen(l_refs[g][0, 0, r])
            outs.append(so_ref[g])
            lses.append(sl_ref[g])
    mx = lses[0]
    for l in lses[1:]:
        mx = jnp.maximum(mx, l)
    ws = [jnp.exp(l - mx) for l in lses]
    tot = ws[0]
    for w in ws[1:]:
        tot = tot + w
    acc = None
    for w, o in zip(ws, outs):
        term = (w / tot) * o
        acc = term if acc is None else acc + term
    out_ref[0] = acc.astype(out_ref.dtype)


def _dil_mix(outs, lses, dils, tt=1024):
    b, _, _, cw = outs[0].shape
    s = outs[0].shape[1] * outs[0].shape[2]
    npair = cw // LANES
    ng = len(dils)
    o_specs = [pl.BlockSpec((1, d, tt // d, LANES), lambda bi, i, c: (bi, 0, i, c)) for d in dils]
    l_specs = [pl.BlockSpec((1, 1, d, tt // d, 2), lambda bi, i, c: (c, bi, 0, i, 0)) for d in dils]
    return pl.pallas_call(
        functools.partial(_dil_mix_kernel, dils=tuple(dils)),
        grid=(b, s // tt, npair),
        in_specs=o_specs + l_specs,
        out_specs=pl.BlockSpec((1, tt, LANES), lambda bi, i, c: (bi, i, c)),
        out_shape=jax.ShapeDtypeStruct((b, s, cw), BF16),
        scratch_shapes=[pltpu.VMEM((ng, tt, LANES), F32), pltpu.VMEM((ng, tt, LANES), F32)],
        compiler_params=_cparams(("arbitrary", "arbitrary", "arbitrary")),
        name="dil_mix",
    )(*outs, *lses)


def _norm_router_kernel(x_ref, g_ref, scale_ref, shift_ref, rw_ref, rb_ref, h_ref, idx_ref, gate_ref):
    h = _rms(x_ref[0], g_ref[...]) * (1.0 + scale_ref[0]) + shift_ref[0]
    h_ref[0] = h
    logits = _dot3(h, rw_ref[...]) + rb_ref[...]
    n_e = logits.shape[1]
    eidx = lax.broadcasted_iota(jnp.int32, logits.shape, 1).astype(F32)
    vals, idxs = [], []
    for _ in range(TOP_K):
        m = jnp.max(logits, axis=-1, keepdims=True)
        first = jnp.min(jnp.where(logits == m, eidx, float(n_e)), axis=-1, keepdims=True)
        vals.append(m)
        idxs.append(first)
        logits = jnp.where(eidx == first, -jnp.inf, logits)
    top = _columns(vals)
    e = jnp.exp(top - vals[0])
    gate_ref[0] = e / jnp.sum(e, axis=-1, keepdims=True)
    idx_ref[0] = _columns(idxs).astype(jnp.int32)


def _norm_router(x, g, scale, shift, rw, rb, tm=512):
    b, s, d = x.shape
    n_e = rw.shape[1]
    tok = lambda w: pl.BlockSpec((1, tm, w), lambda bi, i: (bi, i, 0))
    return pl.pallas_call(
        _norm_router_kernel,
        grid=(b, s // tm),
        in_specs=[
            tok(d),
            pl.BlockSpec((1, d), lambda bi, i: (0, 0)),
            pl.BlockSpec((1, 1, d), lambda bi, i: (bi, 0, 0)),
            pl.BlockSpec((1, 1, d), lambda bi, i: (bi, 0, 0)),
            pl.BlockSpec((d, n_e), lambda bi, i: (0, 0)),
            pl.BlockSpec((1, n_e), lambda bi, i: (0, 0)),
        ],
        out_specs=[tok(d), tok(TOP_K), tok(TOP_K)],
        out_shape=[
            jax.ShapeDtypeStruct((b, s, d), F32),
            jax.ShapeDtypeStruct((b, s, TOP_K), jnp.int32),
            jax.ShapeDtypeStruct((b, s, TOP_K), F32),
        ],
        compiler_params=_cparams(("arbitrary", "arbitrary")),
        name="norm_router",
    )(x, g.reshape(1, d), scale, shift, rw, rb.reshape(1, n_e))


def _moe_plan(top_idx, n_tok):
    n_assign = n_tok * TOP_K
    flat_e = top_idx.reshape(-1)
    onehot = (flat_e[:, None] == jnp.arange(N_EXPERTS, dtype=jnp.int32)[None, :]).astype(jnp.int32)
    csum = jnp.cumsum(onehot, axis=0)
    counts = csum[-1]
    rank = jnp.take_along_axis(csum, flat_e[:, None], axis=1)[:, 0] - 1
    padded = (counts + MOE_SUB - 1) // MOE_SUB * MOE_SUB
    pstarts = jnp.cumsum(padded) - padded
    dest = pstarts[flat_e] + rank
    n_slots = n_assign + N_EXPERTS * MOE_SUB
    slot_tok = jnp.zeros((n_slots,), jnp.int32).at[dest].set(
        jnp.arange(n_assign, dtype=jnp.int32) // TOP_K)
    n_items = (n_slots + N_EXPERTS * (MOE_ITEM_ROWS - MOE_SUB)) // MOE_ITEM_ROWS
    items_per_e = (padded + MOE_ITEM_ROWS - 1) // MOE_ITEM_ROWS
    item_ends = jnp.cumsum(items_per_e)
    total_items = item_ends[-1]
    i = jnp.arange(n_items, dtype=jnp.int32)
    i_eff = jnp.minimum(i, total_items - 1)
    e_i = jnp.clip(jnp.searchsorted(item_ends, i_eff, side='right'), 0, N_EXPERTS - 1).astype(jnp.int32)
    local = i_eff - (item_ends[e_i] - items_per_e[e_i])
    row0 = pstarts[e_i] + local * MOE_ITEM_ROWS
    nrows = jnp.clip(padded[e_i] - local * MOE_ITEM_ROWS, 0, MOE_ITEM_ROWS)
    nsub = jnp.where(i < total_items, nrows // MOE_SUB, 0)
    return (e_i.astype(jnp.int32), row0.astype(jnp.int32), nsub.astype(jnp.int32),
            slot_tok, dest.astype(jnp.int32))


def _swiglu_pair(h_a, h_b):
    n = h_a.shape[1]
    lane = lax.broadcasted_iota(jnp.int32, (1, n), 1)
    even = (lane % 2) == 0

    def glu_lin(h):
        glu = jnp.minimum(h, SWIGLU_LIMIT)
        glu = glu * jax.nn.sigmoid(SWIGLU_ALPHA * glu)
        lin = jnp.clip(h, -SWIGLU_LIMIT, SWIGLU_LIMIT) + 1.0
        return glu, lin

    glu_a, lin_a = glu_lin(h_a)
    glu_b, lin_b = glu_lin(h_b)
    act_a = glu_a * pltpu.roll(lin_a, n - 1, 1)
    act_b = pltpu.roll(glu_b, 1, 1) * lin_b
    return jnp.where(even, act_a, act_b)


def _dispatch_kernel(stok_ref, h_hbm, o_ref, buf_ref, sem):
    i = pl.program_id(0)
    n = pl.num_programs(0)
    tt = o_ref.shape[0]

    def issue(step):
        sl = step % 2

        def body(u, carry):
            for q in range(DMA_UNROLL):
                r = u * DMA_UNROLL + q
                pltpu.make_async_copy(h_hbm.at[stok_ref[step * tt + r]], buf_ref.at[sl, r], sem.at[sl]).start()
            return carry

        lax.fori_loop(0, tt // DMA_UNROLL, body, 0)

    @pl.when(i == 0)
    def _():
        issue(i)

    @pl.when(i + 1 < n)
    def _():
        issue(i + 1)

    slot = i % 2
    pltpu.make_async_copy(buf_ref.at[slot], buf_ref.at[slot], sem.at[slot]).wait()
    o_ref[...] = buf_ref[slot]


def _moe_dispatch(h, slot_tok):
    n_slots = slot_tok.shape[0]
    d = h.shape[1]
    tt = DISPATCH_ROWS
    assert n_slots % tt == 0
    grid_spec = pltpu.PrefetchScalarGridSpec(
        num_scalar_prefetch=1,
        grid=(n_slots // tt,),
        in_specs=[pl.BlockSpec(memory_space=pl.ANY)],
        out_specs=pl.BlockSpec((tt, d), lambda i, st: (i, 0)),
        scratch_shapes=[pltpu.VMEM((2, tt, d), h.dtype), pltpu.SemaphoreType.DMA((2,))],
    )
    return pl.pallas_call(
        _dispatch_kernel,
        grid_spec=grid_spec,
        out_shape=jax.ShapeDtypeStruct((n_slots, d), h.dtype),
        compiler_params=_cparams(("arbitrary",)),
        name="moe_dispatch",
    )(slot_tok, h)


def _for_each_sub(nsub, fn):
    def pair(p, carry):
        fn(2 * p)
        fn(2 * p + 1)
        return carry

    lax.fori_loop(0, nsub // 2, pair, 0)

    @pl.when(nsub % 2 == 1)
    def _():
        fn(nsub - 1)


def _moe_kernel(ie_ref, row0_ref, nsub_ref,
                xs_hbm, wgu_a_ref, wgu_b_ref, bgu_a_ref, bgu_b_ref, wdn_ref, bdn_ref,
                y_hbm,
                xs_ref, act_ref, ot_ref, wgu_bf_ref, il_ref, wdn_bf_ref, xsem, osem, *, n_gu):
    i = pl.program_id(0)
    s = pl.program_id(1)
    n_items = pl.num_programs(0)
    n_steps = pl.num_programs(1)
    nsub = nsub_ref[i]
    row0 = row0_ref[i]
    slot = i % 2
    ft2 = 2 * MOE_FT

    def sub_rows(sb):
        return pl.ds(pl.multiple_of(sb * MOE_SUB, MOE_SUB), MOE_SUB)

    def xs_copy(item, sb):
        src0 = pl.multiple_of(row0_ref[item] + sb * MOE_SUB, MOE_SUB)
        return pltpu.make_async_copy(xs_hbm.at[pl.ds(src0, MOE_SUB)],
                                     xs_ref.at[item % 2, pl.ds(sb * MOE_SUB, MOE_SUB)],
                                     xsem.at[item % 2])

    def for_item_subs(item, fn):
        for sb in range(MOE_NSUB):
            @pl.when(sb < nsub_ref[item])
            def _():
                fn(sb)

    @pl.when((s == 0) & (nsub > 0))
    def _fetch_rows():
        @pl.when(i == 0)
        def _():
            for_item_subs(i, lambda sb: xs_copy(i, sb).start())

        for_item_subs(i, lambda sb: xs_copy(i, sb).wait())

        @pl.when(i + 1 < n_items)
        def _():
            for_item_subs(i + 1, lambda sb: xs_copy(i + 1, sb).start())

    @pl.when((s < n_gu) & (nsub > 0))
    def _gate_up():
        wgu_bf_ref[:, 0:ft2] = wgu_a_ref[0, 0].astype(BF16)
        wgu_bf_ref[:, ft2:2 * ft2] = wgu_b_ref[0, 0].astype(BF16)
        b_a = bgu_a_ref[0, 0]
        b_b = bgu_b_ref[0, 0]

        def one(sb):
            rows = sub_rows(sb)
            x = xs_ref[slot, rows, :].astype(BF16)
            hgu = _dot(x, wgu_bf_ref[...])
            act = _swiglu_pair(hgu[:, 0:ft2] + b_a, hgu[:, ft2:2 * ft2] + b_b)
            act_ref[s, rows, :] = act.astype(BF16)

        _for_each_sub(nsub, one)

    def out_copy(osl, n, sb):
        dst0 = pl.multiple_of(row0 + sb * MOE_SUB, MOE_SUB)
        return pltpu.make_async_copy(
            ot_ref.at[osl, pl.ds(sb * MOE_SUB, MOE_SUB)],
            y_hbm.at[pl.ds(dst0, MOE_SUB), pl.ds(pl.multiple_of(n * MOE_TILE, MOE_TILE), MOE_TILE)],
            osem.at[osl])

    @pl.when((s >= n_gu) & (nsub > 0))
    def _down():
        n = s - n_gu
        osl = n % 2

        @pl.when(n >= 2)
        def _():
            for_item_subs(i, lambda sb: out_copy(osl, 0, sb).wait())

        half = wdn_ref.shape[2] // 2
        for c in range(MOE_TILE // LANES):
            cols = slice(c * LANES, (c + 1) * LANES)
            il_ref[c, pl.ds(0, half, stride=2), :] = wdn_ref[0, 0, 0:half, cols]
            il_ref[c, pl.ds(1, half, stride=2), :] = wdn_ref[0, 0, half:2 * half, cols]
            wdn_bf_ref[:, cols] = il_ref[c].astype(BF16)
        bias = bdn_ref[0, 0]

        def one(sb):
            rows = sub_rows(sb)
            acc = None
            for f in range(n_gu):
                term = _dot(act_ref[f, rows, :], wdn_bf_ref[f * MOE_TILE:(f + 1) * MOE_TILE, :])
                acc = term if acc is None else acc + term
            ot_ref[osl, rows, :] = acc + bias

        _for_each_sub(nsub, one)
        for_item_subs(i, lambda sb: out_copy(osl, n, sb).start())

        @pl.when(s == n_steps - 1)
        def _():
            for_item_subs(i, lambda sb: out_copy(1 - osl, 0, sb).wait())
            for_item_subs(i, lambda sb: out_copy(osl, 0, sb).wait())

def _moe_experts(xs, plan, w_gu, b_gu, w_dn, b_dn, layer):
    n_slots, d = xs.shape
    _, n_e, _, two_f = w_gu.shape
    d_ff = two_f // 2
    item_e, item_row0, item_nsub = plan
    n_items = item_e.shape[0]
    n_gu = (d_ff // 2) // MOE_FT
    n_dn = d // MOE_TILE
    ft2 = 2 * MOE_FT
    assert w_dn.shape[2] == n_gu * MOE_TILE

    def gu_tile(i, s, ns):
        return jnp.where(ns[i] > 0, jnp.minimum(s, n_gu - 1), n_gu - 1)

    def dn_tile(i, s, ns):
        return jnp.where(ns[i] > 0, jnp.maximum(s - n_gu, 0), n_dn - 1)

    def gu_map(half):
        return lambda i, s, ie, r0, ns: (layer, ie[i], 0, half * n_gu + gu_tile(i, s, ns))

    dn_map = lambda i, s, ie, r0, ns: (layer, ie[i], 0, dn_tile(i, s, ns))

    grid_spec = pltpu.PrefetchScalarGridSpec(
        num_scalar_prefetch=3,
        grid=(n_items, n_gu + n_dn),
        in_specs=[
            pl.BlockSpec(memory_space=pl.ANY),
            pl.BlockSpec((1, 1, d, ft2), gu_map(0)),
            pl.BlockSpec((1, 1, d, ft2), gu_map(1)),
            pl.BlockSpec((1, 1, 1, ft2), gu_map(0)),
            pl.BlockSpec((1, 1, 1, ft2), gu_map(1)),
            pl.BlockSpec((1, 1, d_ff, MOE_TILE), dn_map),
            pl.BlockSpec((1, 1, 1, MOE_TILE), dn_map),
        ],
        out_specs=pl.BlockSpec(memory_space=pl.ANY),
        scratch_shapes=[
            pltpu.VMEM((2, MOE_ITEM_ROWS, d), F32),
            pltpu.VMEM((n_gu, MOE_ITEM_ROWS, MOE_TILE), BF16),
            pltpu.VMEM((2, MOE_ITEM_ROWS, MOE_TILE), F32),
            pltpu.VMEM((d, 2 * ft2), BF16),
            pltpu.VMEM((MOE_TILE // LANES, d_ff, LANES), F32),
            pltpu.VMEM((d_ff, MOE_TILE), BF16),
            pltpu.SemaphoreType.DMA((2,)),
            pltpu.SemaphoreType.DMA((2,)),
        ],
    )
    depth = w_gu.shape[0]
    return pl.pallas_call(
        functools.partial(_moe_kernel, n_gu=n_gu),
        grid_spec=grid_spec,
        out_shape=jax.ShapeDtypeStruct((n_slots, d), F32),
        input_output_aliases={3: 0},
        compiler_params=_cparams(("arbitrary", "arbitrary")),
        name="moe_experts",
    )(item_e, item_row0, item_nsub, xs, w_gu, w_gu,
      b_gu.reshape(depth, n_e, 1, two_f), b_gu.reshape(depth, n_e, 1, two_f),
      w_dn, b_dn.reshape(depth, n_e, 1, d))


def _moe_combine_kernel(pos_ref, x_ref, g_ref, gate_ref, y_hbm, o_ref, ybuf_ref, sem):
    i = pl.program_id(0)
    n = pl.num_programs(0)
    tt = x_ref.shape[0]

    def issue(step):
        sl = step % 2

        def body(u, carry):
            for q in range(DMA_UNROLL // TOP_K):
                t = u * (DMA_UNROLL // TOP_K) + q
                for k in range(TOP_K):
                    p = pos_ref[(step * tt + t) * TOP_K + k]
                    pltpu.make_async_copy(y_hbm.at[p], ybuf_ref.at[sl, k, t], sem.at[sl]).start()
            return carry

        lax.fori_loop(0, tt * TOP_K // DMA_UNROLL, body, 0)

    @pl.when(i == 0)
    def _():
        issue(i)

    @pl.when(i + 1 < n)
    def _():
        issue(i + 1)

    slot = i % 2
    pltpu.make_async_copy(ybuf_ref.at[slot], ybuf_ref.at[slot], sem.at[slot]).wait()
    gates = g_ref[...]
    tot = None
    for k in range(TOP_K):
        term = gates[:, k:k + 1] * ybuf_ref[slot, k]
        tot = term if tot is None else tot + term
    o_ref[...] = x_ref[...] + gate_ref[0] * tot


def _moe_combine(x, y, pos, gates, gate_mod):
    b, s, d = x.shape
    tt = COMBINE_TOKENS
    n_tok = b * s
    grid_spec = pltpu.PrefetchScalarGridSpec(
        num_scalar_prefetch=1,
        grid=(n_tok // tt,),
        in_specs=[
            pl.BlockSpec((tt, d), lambda i, pos_r: (i, 0)),
            pl.BlockSpec((tt, TOP_K), lambda i, pos_r: (i, 0)),
            pl.BlockSpec((1, 1, d), lambda i, pos_r: (i * tt // s, 0, 0)),
            pl.BlockSpec(memory_space=pl.ANY),
        ],
        out_specs=pl.BlockSpec((tt, d), lambda i, pos_r: (i, 0)),
        scratch_shapes=[pltpu.VMEM((2, TOP_K, tt, d), F32), pltpu.SemaphoreType.DMA((2,))],
    )
    out = pl.pallas_call(
        _moe_combine_kernel,
        grid_spec=grid_spec,
        out_shape=jax.ShapeDtypeStruct((n_tok, d), F32),
        compiler_params=_cparams(("arbitrary",)),
        name="moe_combine",
    )(pos, x.reshape(n_tok, d), gates.reshape(n_tok, TOP_K), gate_mod, y)
    return out.reshape(b, s, d)


def _moe_ffn(x, g, scale, shift, gate_mod, rw, rb, w_gu, b_gu, w_dn, b_dn, layer):
    b, s, d = x.shape
    h, top_idx, gates = _norm_router(x, g, scale, shift, rw, rb)
    item_e, item_row0, item_nsub, slot_tok, pos = _moe_plan(top_idx.reshape(b * s, TOP_K), b * s)
    xs = _moe_dispatch(h.reshape(b * s, d), slot_tok)
    y = _moe_experts(xs, (item_e, item_row0, item_nsub), w_gu, b_gu, w_dn, b_dn, layer)
    return _moe_combine(x, y, pos, gates, gate_mod)


def _rope_lane_tables(seq):
    inv_freq = ROPE_THETA ** (-jnp.arange(0, HEAD_DIM, 2, dtype=F32) / HEAD_DIM)
    ang = jnp.arange(seq, dtype=F32)[:, None] * inv_freq[None, :]
    cos, sin = jnp.cos(ang), jnp.sin(ang)
    reps = LANES // HEAD_DIM
    return (jnp.tile(jnp.concatenate([cos, cos], axis=-1), (1, reps)),
            jnp.tile(jnp.concatenate([-sin, sin], axis=-1), (1, reps)))


def _col_modes(kinds, tn, width):
    modes = []
    for n_cols, mode in kinds:
        assert n_cols % tn == 0
        modes += [mode] * (n_cols // tn)
    assert len(modes) * tn == width
    return jnp.asarray(modes, jnp.int32)


PLAIN, ROPE, SCALE, ROPE_SCALE = 0, 1, 2, 3


def kernel(x, c, ada_w, ada_b, norm_g, final_g, ab_w_in, ab_w_out, na_rpb, diff_lam_q1, diff_lam_k1,
           diff_lam_q2, diff_lam_k2, diff_subln_g, dil_w_in, dil_w_out, moe_router_w, moe_router_b,
           moe_w_gate_up, moe_b_gate_up, moe_w_down, moe_b_down):
    b, s, d = x.shape
    depth = ada_w.shape[0]
    tn = 512
    cos_t, sin_t = _rope_lane_tables(s)

    c_pad = jnp.zeros((8, d), F32).at[:b].set(c)
    mod = _adaln(c_pad, ada_w.reshape(depth * 2, d, 3 * d), ada_b.reshape(depth * 2, 3 * d))
    mod = mod[:, :b].reshape(depth, 2, b, 3, 1, d)

    na_w = NA_HEADS * HEAD_DIM
    df_w = DIFF_HEADS * 2 * HEAD_DIM
    dil_w = DIL_HEADS * HEAD_DIM

    for l in range(depth):
        shift, scale, gate = (mod[l, 0, :, t] for t in range(3))
        h = _norm_mod(x, norm_g[l, 0], scale, shift, BF16)
        i = l // 2
        if l % 2 == 0:
            lambda_init = 0.8 - 0.6 * math.exp(-0.3 * l)
            modes = _col_modes([(na_w, SCALE), (2 * na_w, PLAIN), (df_w, ROPE_SCALE), (df_w, ROPE),
                                (df_w, PLAIN)], tn, ab_w_in.shape[2])
            proj = _proj(h, ab_w_in[i], modes, cos_t, sin_t, ab_w_in.shape[2], tn=tn)
            proj = proj.reshape(b, s, ab_w_in.shape[2])
            o_na = _na_attention(proj, _na_bias_table(na_rpb[i], s // GRID_W))
            o_df = _diff_attention(proj, diff_lam_q1[i], diff_lam_k1[i], diff_lam_q2[i], diff_lam_k2[i],
                                   diff_subln_g[i], lambda_init, col0=3 * na_w // LANES)
            o = jnp.concatenate([o_na, o_df], axis=-1)
            x = _outproj(o, ab_w_out[i], x, gate)
        else:
            modes = _col_modes([(dil_w, ROPE_SCALE), (dil_w, ROPE), (dil_w, PLAIN)] * len(DIL_CONFIGS),
                               tn, dil_w_in.shape[2])
            outs, lses = [], []
            for gi, (window, dil) in enumerate(DIL_CONFIGS):
                proj = _proj(h, dil_w_in[i], modes, cos_t, sin_t, 3 * dil_w, jbase=gi * (3 * dil_w // tn),
                             dil=dil, tn=tn)
                o_g, lse_g = _dil_attention(proj, window, dil)
                outs.append(o_g)
                lses.append(lse_g)
            o = _dil_mix(outs, lses, [dil for _, dil in DIL_CONFIGS])
            x = _outproj(o, dil_w_out[i], x, gate)
        shift, scale, gate = (mod[l, 1, :, t] for t in range(3))
        x = _moe_ffn(x, norm_g[l, 1], scale, shift, gate, moe_router_w[l], moe_router_b[l],
                     moe_w_gate_up, moe_b_gate_up, moe_w_down, moe_b_down, l)
    return _final_norm(x, final_g)
```

```python
import functools
import math

import jax
import jax.numpy as jnp
from jax import lax
from jax.experimental import pallas as pl
from jax.experimental.pallas import tpu as pltpu

F32 = jnp.float32
BF16 = jnp.bfloat16

HEAD_DIM = 64
LANES = 128
GRID_W = 64
NA_HEADS = 16
NA_ROWS = 8
NA_COLS = 16
DIFF_HEADS = 8
DIL_HEADS = 32
DIL_CONFIGS = ((128, 1), (512, 4), (2048, 16))
ROPE_THETA = 10000.0
N_EXPERTS = 32
TOP_K = 4
SWIGLU_LIMIT = 7.0
SWIGLU_ALPHA = 1.702
RMS_EPS = 1e-5

PROJ_MCHUNKS = 4
DIL_QBLOCK = 128
DIL_BLOCK_UNROLL = 4
NA_ROW_UNROLL = 4
MOE_SUB = 256
MOE_NSUB = 5
MOE_ITEM_ROWS = MOE_NSUB * MOE_SUB
MOE_FT = 128
MOE_TILE = 2 * MOE_FT
DISPATCH_ROWS = 256
DMA_UNROLL = 8
COMBINE_TOKENS = 128
VMEM_LIMIT = 56 * 1024 * 1024


def _cparams(sem, vmem=VMEM_LIMIT):
    return pltpu.CompilerParams(dimension_semantics=sem, vmem_limit_bytes=vmem)


def _dot(a, b):
    return jnp.dot(a, b, preferred_element_type=F32)


def _dot_nt(a, b):
    return lax.dot_general(a, b, (((1,), (1,)), ((), ())), preferred_element_type=F32)


def _split_bf16(a):
    hi = a.astype(BF16)
    lo = (a - hi.astype(F32)).astype(BF16)
    return hi, lo


def _dot3(a, b):
    ah, al = _split_bf16(a)
    bh, bl = _split_bf16(b)
    return _dot(ah, bh) + (_dot(ah, bl) + _dot(al, bh))


def _adaln_kernel(c_ref, w_ref, b_ref, o_ref):
    c = c_ref[...]
    c_act = c * jax.nn.sigmoid(c)
    o_ref[0] = _dot3(c_act, w_ref[0]) + b_ref[0]


def _adaln(c_pad, w, b, tn=768):
    n_mod, d, n = w.shape
    return pl.pallas_call(
        _adaln_kernel,
        grid=(n_mod, n // tn),
        in_specs=[
            pl.BlockSpec(c_pad.shape, lambda l, j: (0, 0)),
            pl.BlockSpec((1, d, tn), lambda l, j: (l, 0, j)),
            pl.BlockSpec((1, 1, tn), lambda l, j: (l, 0, j)),
        ],
        out_specs=pl.BlockSpec((1, c_pad.shape[0], tn), lambda l, j: (l, 0, j)),
        out_shape=jax.ShapeDtypeStruct((n_mod, c_pad.shape[0], n), F32),
        compiler_params=_cparams(("arbitrary", "arbitrary")),
        name="adaln",
    )(c_pad, w, b.reshape(n_mod, 1, n))


def _rms(x, g):
    return x * lax.rsqrt(jnp.mean(x * x, axis=-1, keepdims=True) + RMS_EPS) * g


def _norm_mod_kernel(x_ref, g_ref, scale_ref, shift_ref, o_ref):
    h = _rms(x_ref[0], g_ref[...]) * (1.0 + scale_ref[0]) + shift_ref[0]
    o_ref[0] = h.astype(o_ref.dtype)


def _norm_mod(x, g, scale, shift, out_dtype, tm=512):
    b, s, d = x.shape
    return pl.pallas_call(
        _norm_mod_kernel,
        grid=(b, s // tm),
        in_specs=[
            pl.BlockSpec((1, tm, d), lambda bi, i: (bi, i, 0)),
            pl.BlockSpec((1, d), lambda bi, i: (0, 0)),
            pl.BlockSpec((1, 1, d), lambda bi, i: (bi, 0, 0)),
            pl.BlockSpec((1, 1, d), lambda bi, i: (bi, 0, 0)),
        ],
        out_specs=pl.BlockSpec((1, tm, d), lambda bi, i: (bi, i, 0)),
        out_shape=jax.ShapeDtypeStruct((b, s, d), out_dtype),
        compiler_params=_cparams(("arbitrary", "arbitrary")),
        name="norm_mod",
    )(x, g.reshape(1, d), scale, shift)


def _final_norm_kernel(x_ref, g_ref, o_ref):
    o_ref[0] = _rms(x_ref[0], g_ref[...])


def _final_norm(x, g, tm=512):
    b, s, d = x.shape
    return pl.pallas_call(
        _final_norm_kernel,
        grid=(b, s // tm),
        in_specs=[
            pl.BlockSpec((1, tm, d), lambda bi, i: (bi, i, 0)),
            pl.BlockSpec((1, d), lambda bi, i: (0, 0)),
        ],
        out_specs=pl.BlockSpec((1, tm, d), lambda bi, i: (bi, i, 0)),
        out_shape=jax.ShapeDtypeStruct((b, s, d), F32),
        compiler_params=_cparams(("arbitrary", "arbitrary")),
        name="final_norm",
    )(x, g.reshape(1, d))


def _proj_kernel(mode_ref, a_ref, w_ref, cos_ref, sin_ref, o_ref, wbf_ref, stage_ref, *, jbase, dil):
    j = pl.program_id(0)

    @pl.when((pl.program_id(1) == 0) & (pl.program_id(2) == 0))
    def _():
        wbf_ref[...] = w_ref[...].astype(BF16)

    mode = mode_ref[jbase + j]
    tm = a_ref.shape[1]
    tn = wbf_ref.shape[1]
    cm = tm // PROJ_MCHUNKS

    def emit(mi, ci, val):
        cols = slice(ci * LANES, (ci + 1) * LANES)
        if dil == 1:
            o_ref[0, 0, mi * cm:(mi + 1) * cm, cols] = val.astype(o_ref.dtype)
        else:
            stage_ref[ci, mi * cm:(mi + 1) * cm, :] = val
            n = cm // dil
            for r in range(dil):
                o_ref[0, r, mi * n:(mi + 1) * n, cols] = (
                    stage_ref[ci, pl.ds(mi * cm + r, n, stride=dil), :].astype(o_ref.dtype))

    @pl.when(mode == 0)
    def _():
        for mi in range(PROJ_MCHUNKS):
            acc = _dot(a_ref[0, mi * cm:(mi + 1) * cm, :], wbf_ref[...])
            for ci in range(tn // LANES):
                emit(mi, ci, acc[:, ci * LANES:(ci + 1) * LANES])

    @pl.when(mode != 0)
    def _():
        rope_on = (mode & 1).astype(F32)
        qscale = jnp.where((mode & 2) == 2, HEAD_DIM ** -0.5, 1.0).astype(F32)
        lane = lax.broadcasted_iota(jnp.int32, (1, LANES), 1)
        first_half = (lane % HEAD_DIM) < (HEAD_DIM // 2)
        for mi in range(PROJ_MCHUNKS):
            rows = slice(mi * cm, (mi + 1) * cm)
            acc = _dot(a_ref[0, rows, :], wbf_ref[...])
            cos = (cos_ref[rows, :] * rope_on + (1.0 - rope_on)) * qscale
            sin = sin_ref[rows, :] * (rope_on * qscale)
            for ci in range(tn // LANES):
                x = acc[:, ci * LANES:(ci + 1) * LANES]
                partner = jnp.where(first_half,
                                    pltpu.roll(x, LANES - HEAD_DIM // 2, 1),
                                    pltpu.roll(x, HEAD_DIM // 2, 1))
                emit(mi, ci, x * cos + partner * sin)


def _proj(a, w, modes, cos_t, sin_t, n_out, jbase=0, dil=1, tm=1024, tn=512):
    b, s, k = a.shape
    grid_spec = pltpu.PrefetchScalarGridSpec(
        num_scalar_prefetch=1,
        grid=(n_out // tn, b, s // tm),
        in_specs=[
            pl.BlockSpec((1, tm, k), lambda j, bi, i, m: (bi, i, 0)),
            pl.BlockSpec((k, tn), lambda j, bi, i, m: (0, jbase + j)),
            pl.BlockSpec((tm, LANES), lambda j, bi, i, m: (i, 0)),
            pl.BlockSpec((tm, LANES), lambda j, bi, i, m: (i, 0)),
        ],
        out_specs=pl.BlockSpec((1, dil, tm // dil, tn), lambda j, bi, i, m: (bi, 0, i, j)),
        scratch_shapes=[pltpu.VMEM((k, tn), BF16), pltpu.VMEM((tn // LANES, tm, LANES), F32)],
    )
    return pl.pallas_call(
        functools.partial(_proj_kernel, jbase=jbase, dil=dil),
        grid_spec=grid_spec,
        out_shape=jax.ShapeDtypeStruct((b, dil, s // dil, n_out), BF16),
        compiler_params=_cparams(("arbitrary", "arbitrary", "arbitrary")),
        name="proj",
    )(modes, a, w, cos_t, sin_t)


def _outproj_kernel(a_ref, w_ref, x_ref, gate_ref, o_ref, wbf_ref):
    @pl.when((pl.program_id(1) == 0) & (pl.program_id(2) == 0))
    def _():
        wbf_ref[...] = w_ref[...].astype(BF16)

    o_ref[0] = x_ref[0] + gate_ref[0] * _dot(a_ref[0], wbf_ref[...])


def _outproj(a, w, x, gate, tm=1024, tn=512):
    b, s, k = a.shape
    n = w.shape[1]
    return pl.pallas_call(
        _outproj_kernel,
        grid=(n // tn, b, s // tm),
        in_specs=[
            pl.BlockSpec((1, tm, k), lambda j, bi, i: (bi, i, 0)),
            pl.BlockSpec((k, tn), lambda j, bi, i: (0, j)),
            pl.BlockSpec((1, tm, tn), lambda j, bi, i: (bi, i, j)),
            pl.BlockSpec((1, 1, tn), lambda j, bi, i: (bi, 0, j)),
        ],
        out_specs=pl.BlockSpec((1, tm, tn), lambda j, bi, i: (bi, i, j)),
        out_shape=jax.ShapeDtypeStruct((b, s, n), F32),
        scratch_shapes=[pltpu.VMEM((k, tn), BF16)],
        compiler_params=_cparams(("arbitrary", "arbitrary", "arbitrary")),
        name="outproj",
    )(a, w, x, gate)


def _columns(cols):
    m = cols[0].shape[0]
    lane = lax.broadcasted_iota(jnp.int32, (m, len(cols)), 1)
    out = jnp.broadcast_to(cols[0], (m, len(cols)))
    for k in range(1, len(cols)):
        out = jnp.where(lane == k, cols[k], out)
    return out


def _head_masks():
    lane = lax.broadcasted_iota(jnp.int32, (1, LANES), 1)
    return lane < HEAD_DIM, lane >= HEAD_DIM


def _na_kernel(q_ref, k_ref, v_ref, bias_ref, o_ref, *, rows):
    kr = min(NA_ROWS, rows)
    masks = _head_masks()

    def body(t, carry):
        r_list = [t * NA_ROW_UNROLL + u for u in range(NA_ROW_UNROLL)]
        scores, values = [], []
        for r in r_list:
            rs = jnp.clip(r - kr // 2, 0, rows - kr)
            var = r - rs
            q = q_ref[0, pl.ds(pl.multiple_of(r * GRID_W, GRID_W), GRID_W), :]
            k = k_ref[0, pl.ds(pl.multiple_of(rs * GRID_W, GRID_W), kr * GRID_W), :]
            values.append(v_ref[0, pl.ds(pl.multiple_of(rs * GRID_W, GRID_W), kr * GRID_W), :])
            for h in range(2):
                qh = jnp.where(masks[h], q, jnp.zeros_like(q))
                scores.append(_dot_nt(qh, k) + bias_ref[h, var])
        probs, dens = [], []
        for sc in scores:
            e = jnp.exp(sc - jnp.max(sc, axis=-1, keepdims=True))
            dens.append(jnp.sum(e, axis=-1, keepdims=True))
            probs.append(e.astype(BF16))
        for u, r in enumerate(r_list):
            outs = [_dot(probs[2 * u + h], values[u]) / dens[2 * u + h] for h in range(2)]
            o = jnp.where(masks[0], outs[0], outs[1])
            o_ref[0, pl.ds(pl.multiple_of(r * GRID_W, GRID_W), GRID_W), :] = o.astype(o_ref.dtype)
        return carry

    assert rows % NA_ROW_UNROLL == 0
    lax.fori_loop(0, rows // NA_ROW_UNROLL, body, 0)


def _na_bias_table(rpb, rows):
    kr = min(NA_ROWS, rows)
    c = jnp.arange(GRID_W)
    kc = jnp.arange(GRID_W)
    cs = jnp.clip(c - NA_COLS // 2, 0, GRID_W - NA_COLS)
    valid = (kc[None, :] >= cs[:, None]) & (kc[None, :] < cs[:, None] + NA_COLS)
    coff = jnp.clip(kc[None, :] - c[:, None] + (NA_COLS - 1), 0, 2 * NA_COLS - 2)
    rows_v = jnp.stack([rpb[:, NA_ROWS - 1 - v:NA_ROWS - 1 - v + kr, :] for v in range(kr)], axis=1)
    onehot = ((coff[None] == jnp.arange(2 * NA_COLS - 1)[:, None, None]) & valid[None]).astype(F32)
    tbl = jnp.einsum('hvio,ock->hvcik', rows_v.astype(F32), onehot, precision=lax.Precision.HIGHEST)
    tbl = jnp.where(valid[None, None, :, None, :], tbl, -jnp.inf)
    return tbl.reshape(rpb.shape[0], kr, GRID_W, kr * GRID_W)


def _na_attention(proj, bias_tbl):
    b, s, _ = proj.shape
    rows = s // GRID_W
    npair = NA_HEADS // 2
    kr = bias_tbl.shape[1]
    blk = lambda off: pl.BlockSpec((1, s, LANES), lambda bi, c: (bi, 0, off + c))
    return pl.pallas_call(
        functools.partial(_na_kernel, rows=rows),
        grid=(b, npair),
        in_specs=[
            blk(0), blk(npair), blk(2 * npair),
            pl.BlockSpec((2, kr, GRID_W, kr * GRID_W), lambda bi, c: (c, 0, 0, 0)),
        ],
        out_specs=pl.BlockSpec((1, s, LANES), lambda bi, c: (bi, 0, c)),
        out_shape=jax.ShapeDtypeStruct((b, s, NA_HEADS * HEAD_DIM), BF16),
        compiler_params=_cparams(("arbitrary", "arbitrary")),
        name="na_attn",
    )(proj, proj, proj, bias_tbl)


def _diff_kernel(lam_ref, q_ref, k_ref, v_ref, g_ref, o_ref, *, tk, lambda_init):
    masks = _head_masks()
    q = q_ref[0]
    tq = q.shape[0]
    s = k_ref.shape[1]
    qs = [jnp.where(masks[m], q, jnp.zeros_like(q)) for m in range(2)]

    def body(j, carry):
        k = k_ref[0, pl.ds(pl.multiple_of(j * tk, tk), tk), :]
        v = v_ref[0, pl.ds(pl.multiple_of(j * tk, tk), tk), :]
        scores = [_dot_nt(qs[m], k) for m in range(2)]
        stats = []
        for m in range(2):
            m_old, l_old = carry[3 * m], carry[3 * m + 1]
            m_new = jnp.maximum(m_old, jnp.max(scores[m], axis=-1, keepdims=True))
            alpha = jnp.exp(m_old - m_new)
            e = jnp.exp(scores[m] - m_new)
            l_new = alpha * l_old + jnp.sum(e, axis=-1, keepdims=True)
            stats.append((m_new, l_new, alpha, e.astype(BF16)))
        new = []
        for m in range(2):
            m_new, l_new, alpha, p = stats[m]
            new += [m_new, l_new, alpha * carry[3 * m + 2] + _dot(p, v)]
        return tuple(new)

    init = []
    for m in range(2):
        init += [jnp.full((tq, 1), -jnp.inf, F32), jnp.zeros((tq, 1), F32), jnp.zeros((tq, LANES), F32)]
    res = lax.fori_loop(0, s // tk, body, tuple(init))
    lam = lam_ref[0]
    o = res[2] / res[1] - lam * (res[5] / res[4])
    o_ref[0] = (_rms(o, g_ref[...]) * (1.0 - lambda_init)).astype(o_ref.dtype)


def _diff_lambda_kernel(lq1_ref, lk1_ref, lq2_ref, lk2_ref, o_ref, *, lambda_init):
    s1 = jnp.sum(lq1_ref[...] * lk1_ref[...], axis=-1, keepdims=True)
    s2 = jnp.sum(lq2_ref[...] * lk2_ref[...], axis=-1, keepdims=True)
    o_ref[...] = jnp.exp(s1) - jnp.exp(s2) + lambda_init


def _diff_attention(proj, lq1, lk1, lq2, lk2, subln_g, lambda_init, col0, tq=512, tk=512):
    b, s, _ = proj.shape
    nh = DIFF_HEADS
    vec = lambda a: a.reshape(1, HEAD_DIM).astype(F32)
    lam = pl.pallas_call(
        functools.partial(_diff_lambda_kernel, lambda_init=lambda_init),
        out_shape=jax.ShapeDtypeStruct((1, 1), F32),
        name="diff_lambda",
    )(vec(lq1), vec(lk1), vec(lq2), vec(lk2)).reshape(1)
    grid_spec = pltpu.PrefetchScalarGridSpec(
        num_scalar_prefetch=1,
        grid=(b, nh, s // tq),
        in_specs=[
            pl.BlockSpec((1, tq, LANES), lambda bi, h, i, lam_r: (bi, i, col0 + h)),
            pl.BlockSpec((1, s, LANES), lambda bi, h, i, lam_r: (bi, 0, col0 + nh + h)),
            pl.BlockSpec((1, s, LANES), lambda bi, h, i, lam_r: (bi, 0, col0 + 2 * nh + h)),
            pl.BlockSpec((1, LANES), lambda bi, h, i, lam_r: (0, 0)),
        ],
        out_specs=pl.BlockSpec((1, tq, LANES), lambda bi, h, i, lam_r: (bi, i, h)),
    )
    return pl.pallas_call(
        functools.partial(_diff_kernel, tk=tk, lambda_init=lambda_init),
        grid_spec=grid_spec,
        out_shape=jax.ShapeDtypeStruct((b, s, nh * LANES), BF16),
        compiler_params=_cparams(("arbitrary", "arbitrary", "arbitrary")),
        name="diff_attn",
    )(lam, proj, proj, proj, subln_g.reshape(1, LANES).astype(F32))


def _dil_kernel(q_ref, k_ref, v_ref, o_ref, lse_ref, *, n_side):
    masks = _head_masks()
    length = q_ref.shape[2]
    tq = DIL_QBLOCK
    tk = tq + 2 * n_side
    rel0 = (lax.broadcasted_iota(jnp.int32, (tq, tk), 1)
            - lax.broadcasted_iota(jnp.int32, (tq, tk), 0))

    nq = length // tq
    planes = q_ref.shape[1]

    def body(t, carry):
        where, scores, values = [], [], []
        for u in range(DIL_BLOCK_UNROLL):
            bi = t * DIL_BLOCK_UNROLL + u
            p = bi // nq
            a0 = pl.multiple_of((bi % nq) * tq, tq)
            ks = pl.multiple_of(jnp.clip(a0 - n_side, 0, length - tk), n_side)
            q = q_ref[0, p, pl.ds(a0, tq), :]
            k = k_ref[0, p, pl.ds(ks, tk), :]
            values.append(v_ref[0, p, pl.ds(ks, tk), :])
            where.append((p, a0))
            rel = rel0 + (ks - a0)
            band = (rel >= -n_side) & (rel <= n_side)
            for h in range(2):
                qh = jnp.where(masks[h], q, jnp.zeros_like(q))
                scores.append(jnp.where(band, _dot_nt(qh, k), -jnp.inf))
        probs, dens, lses = [], [], []
        for sc in scores:
            m = jnp.max(sc, axis=-1, keepdims=True)
            e = jnp.exp(sc - m)
            den = jnp.sum(e, axis=-1, keepdims=True)
            probs.append(e.astype(BF16))
            dens.append(den)
            lses.append(m + jnp.log(den))
        for u, (p, a0) in enumerate(where):
            outs = [_dot(probs[2 * u + h], values[u]) / dens[2 * u + h] for h in range(2)]
            o_ref[0, p, pl.ds(a0, tq), :] = jnp.where(masks[0], outs[0], outs[1]).astype(o_ref.dtype)
            lse_ref[0, 0, p, pl.ds(a0, tq), :] = _columns(lses[2 * u:2 * u + 2])
        return carry

    assert (planes * nq) % DIL_BLOCK_UNROLL == 0
    lax.fori_loop(0, planes * nq // DIL_BLOCK_UNROLL, body, 0)


def _dil_attention(proj, window, dil):
    b, _, length, _ = proj.shape
    n_side = (window // 2) // dil
    npair = DIL_HEADS // 2
    cw = DIL_HEADS * HEAD_DIM
    rp = min(dil, max(1, DIL_BLOCK_UNROLL * DIL_QBLOCK // length))
    blk = lambda off: pl.BlockSpec((1, rp, length, LANES), lambda bi, c, r: (bi, r, 0, off + c))
    return pl.pallas_call(
        functools.partial(_dil_kernel, n_side=n_side),
        grid=(b, npair, dil // rp),
        in_specs=[blk(0), blk(npair), blk(2 * npair)],
        out_specs=[
            pl.BlockSpec((1, rp, length, LANES), lambda bi, c, r: (bi, r, 0, c)),
            pl.BlockSpec((1, 1, rp, length, 2), lambda bi, c, r: (c, bi, r, 0, 0)),
        ],
        out_shape=[
            jax.ShapeDtypeStruct((b, dil, length, cw), BF16),
            jax.ShapeDtypeStruct((npair, b, dil, length, 2), F32),
        ],
        compiler_params=_cparams(("arbitrary", "arbitrary", "arbitrary")),
        name="dil_attn",
    )(proj, proj, proj)


def _dil_mix_kernel(*refs, dils):
    ng = len(dils)
    o_refs, l_refs = refs[:ng], refs[ng:2 * ng]
    out_ref, so_ref, sl_ref = refs[2 * ng:]
    masks = _head_masks()
    tt = out_ref.shape[1]

    def widen(lse):
        return jnp.where(masks[0], lse[:, 0:1], lse[:, 1:2])

    outs, lses = [], []
    for g, dil in enumerate(dils):
        if dil == 1:
            outs.append(o_refs[g][0, 0].astype(F32))
            lses.append(widen(l_refs[g][0, 0, 0]))
        else:
            n = tt // dil
            for r in range(dil):
                so_ref[g, pl.ds(r, n, stride=dil), :] = o_refs[g][0, r].astype(F32)
                sl_ref[g, pl.ds(r, n, stride=dil), :] = widen(l_refs[g][0, 0, r])
            outs.append(so_ref[g])
            lses.append(sl_ref[g])
    mx = lses[0]
    for l in lses[1:]:
        mx = jnp.maximum(mx, l)
    ws = [jnp.exp(l - mx) for l in lses]
    tot = ws[0]
    for w in ws[1:]:
        tot = tot + w
    acc = None
    for w, o in zip(ws, outs):
        term = (w / tot) * o
        acc = term if acc is None else acc + term
    out_ref[0] = acc.astype(out_ref.dtype)


def _dil_mix(outs, lses, dils, tt=1024):
    b, _, _, cw = outs[0].shape
    s = outs[0].shape[1] * outs[0].shape[2]
    npair = cw // LANES
    ng = len(dils)
    o_specs = [pl.BlockSpec((1, d, tt // d, LANES), lambda bi, i, c: (bi, 0, i, c)) for d in dils]
    l_specs = [pl.BlockSpec((1, 1, d, tt // d, 2), lambda bi, i, c: (c, bi, 0, i, 0)) for d in dils]
    return pl.pallas_call(
        functools.partial(_dil_mix_kernel, dils=tuple(dils)),
        grid=(b, s // tt, npair),
        in_specs=o_specs + l_specs,
        out_specs=pl.BlockSpec((1, tt, LANES), lambda bi, i, c: (bi, i, c)),
        out_shape=jax.ShapeDtypeStruct((b, s, cw), BF16),
        scratch_shapes=[pltpu.VMEM((ng, tt, LANES), F32), pltpu.VMEM((ng, tt, LANES), F32)],
        compiler_params=_cparams(("arbitrary", "arbitrary", "arbitrary")),
        name="dil_mix",
    )(*outs, *lses)


def _norm_router_kernel(x_ref, g_ref, scale_ref, shift_ref, rw_ref, rb_ref, h_ref, idx_ref, gate_ref):
    h = _rms(x_ref[0], g_ref[...]) * (1.0 + scale_ref[0]) + shift_ref[0]
    h_ref[0] = h
    logits = _dot3(h, rw_ref[...]) + rb_ref[...]
    n_e = logits.shape[1]
    eidx = lax.broadcasted_iota(jnp.int32, logits.shape, 1).astype(F32)
    vals, idxs = [], []
    for _ in range(TOP_K):
        m = jnp.max(logits, axis=-1, keepdims=True)
        first = jnp.min(jnp.where(logits == m, eidx, float(n_e)), axis=-1, keepdims=True)
        vals.append(m)
        idxs.append(first)
        logits = jnp.where(eidx == first, -jnp.inf, logits)
    top = _columns(vals)
    e = jnp.exp(top - vals[0])
    gate_ref[0] = e / jnp.sum(e, axis=-1, keepdims=True)
    idx_ref[0] = _columns(idxs).astype(jnp.int32)


def _norm_router(x, g, scale, shift, rw, rb, tm=512):
    b, s, d = x.shape
    n_e = rw.shape[1]
    tok = lambda w: pl.BlockSpec((1, tm, w), lambda bi, i: (bi, i, 0))
    return pl.pallas_call(
        _norm_router_kernel,
        grid=(b, s // tm),
        in_specs=[
            tok(d),
            pl.BlockSpec((1, d), lambda bi, i: (0, 0)),
            pl.BlockSpec((1, 1, d), lambda bi, i: (bi, 0, 0)),
            pl.BlockSpec((1, 1, d), lambda bi, i: (bi, 0, 0)),
            pl.BlockSpec((d, n_e), lambda bi, i: (0, 0)),
            pl.BlockSpec((1, n_e), lambda bi, i: (0, 0)),
        ],
        out_specs=[tok(d), tok(TOP_K), tok(TOP_K)],
        out_shape=[
            jax.ShapeDtypeStruct((b, s, d), F32),
            jax.ShapeDtypeStruct((b, s, TOP_K), jnp.int32),
            jax.ShapeDtypeStruct((b, s, TOP_K), F32),
        ],
        compiler_params=_cparams(("arbitrary", "arbitrary")),
        name="norm_router",
    )(x, g.reshape(1, d), scale, shift, rw, rb.reshape(1, n_e))


def _moe_plan(top_idx, n_tok):
    n_assign = n_tok * TOP_K
    flat_e = top_idx.reshape(-1)
    onehot = (flat_e[:, None] == jnp.arange(N_EXPERTS, dtype=jnp.int32)[None, :]).astype(jnp.int32)
    csum = jnp.cumsum(onehot, axis=0)
    counts = csum[-1]
    rank = jnp.take_along_axis(csum, flat_e[:, None], axis=1)[:, 0] - 1
    padded = (counts + MOE_SUB - 1) // MOE_SUB * MOE_SUB
    pstarts = jnp.cumsum(padded) - padded
    dest = pstarts[flat_e] + rank
    n_slots = n_assign + N_EXPERTS * MOE_SUB
    slot_tok = jnp.zeros((n_slots,), jnp.int32).at[dest].set(
        jnp.arange(n_assign, dtype=jnp.int32) // TOP_K)
    n_items = (n_slots + N_EXPERTS * (MOE_ITEM_ROWS - MOE_SUB)) // MOE_ITEM_ROWS
    items_per_e = (padded + MOE_ITEM_ROWS - 1) // MOE_ITEM_ROWS
    item_ends = jnp.cumsum(items_per_e)
    total_items = item_ends[-1]
    i = jnp.arange(n_items, dtype=jnp.int32)
    i_eff = jnp.minimum(i, total_items - 1)
    e_i = jnp.clip(jnp.searchsorted(item_ends, i_eff, side='right'), 0, N_EXPERTS - 1).astype(jnp.int32)
    local = i_eff - (item_ends[e_i] - items_per_e[e_i])
    row0 = pstarts[e_i] + local * MOE_ITEM_ROWS
    nrows = jnp.clip(padded[e_i] - local * MOE_ITEM_ROWS, 0, MOE_ITEM_ROWS)
    nsub = jnp.where(i < total_items, nrows // MOE_SUB, 0)
    return (e_i.astype(jnp.int32), row0.astype(jnp.int32), nsub.astype(jnp.int32),
            slot_tok, dest.astype(jnp.int32))


def _swiglu_pair(h_a, h_b):
    n = h_a.shape[1]
    lane = lax.broadcasted_iota(jnp.int32, (1, n), 1)
    even = (lane % 2) == 0

    def glu_lin(h):
        glu = jnp.minimum(h, SWIGLU_LIMIT)
        glu = glu * jax.nn.sigmoid(SWIGLU_ALPHA * glu)
        lin = jnp.clip(h, -SWIGLU_LIMIT, SWIGLU_LIMIT) + 1.0
        return glu, lin

    glu_a, lin_a = glu_lin(h_a)
    glu_b, lin_b = glu_lin(h_b)
    act_a = glu_a * pltpu.roll(lin_a, n - 1, 1)
    act_b = pltpu.roll(glu_b, 1, 1) * lin_b
    return jnp.where(even, act_a, act_b)


def _dispatch_kernel(stok_ref, h_hbm, o_ref, buf_ref, sem):
    i = pl.program_id(0)
    n = pl.num_programs(0)
    tt = o_ref.shape[0]

    def issue(step):
        sl = step % 2

        def body(u, carry):
            for q in range(DMA_UNROLL):
                r = u * DMA_UNROLL + q
                pltpu.make_async_copy(h_hbm.at[stok_ref[step * tt + r]], buf_ref.at[sl, r], sem.at[sl]).start()
            return carry

        lax.fori_loop(0, tt // DMA_UNROLL, body, 0)

    @pl.when(i == 0)
    def _():
        issue(i)

    @pl.when(i + 1 < n)
    def _():
        issue(i + 1)

    slot = i % 2
    pltpu.make_async_copy(buf_ref.at[slot], buf_ref.at[slot], sem.at[slot]).wait()
    o_ref[...] = buf_ref[slot]


def _moe_dispatch(h, slot_tok):
    n_slots = slot_tok.shape[0]
    d = h.shape[1]
    tt = DISPATCH_ROWS
    assert n_slots % tt == 0
    grid_spec = pltpu.PrefetchScalarGridSpec(
        num_scalar_prefetch=1,
        grid=(n_slots // tt,),
        in_specs=[pl.BlockSpec(memory_space=pl.ANY)],
        out_specs=pl.BlockSpec((tt, d), lambda i, st: (i, 0)),
        scratch_shapes=[pltpu.VMEM((2, tt, d), h.dtype), pltpu.SemaphoreType.DMA((2,))],
    )
    return pl.pallas_call(
        _dispatch_kernel,
        grid_spec=grid_spec,
        out_shape=jax.ShapeDtypeStruct((n_slots, d), h.dtype),
        compiler_params=_cparams(("arbitrary",)),
        name="moe_dispatch",
    )(slot_tok, h)


def _for_each_sub(nsub, prep, fn):
    for count in range(1, MOE_NSUB + 1):
        @pl.when(nsub == count)
        def _():
            prep()
            for sb in range(count):
                fn(sb)


def _moe_kernel(ie_ref, row0_ref, nsub_ref,
                xs_hbm, wgu_a_ref, wgu_b_ref, bgu_a_ref, bgu_b_ref, wdn_ref, bdn_ref,
                y_hbm,
                xs_ref, act_ref, ot_ref, wgu_bf_ref, il_ref, wdn_bf_ref, xsem, osem, *, n_gu):
    i = pl.program_id(0)
    s = pl.program_id(1)
    n_items = pl.num_programs(0)
    n_steps = pl.num_programs(1)
    nsub = nsub_ref[i]
    row0 = row0_ref[i]
    slot = i % 2
    ft2 = 2 * MOE_FT

    def sub_rows(sb):
        return slice(sb * MOE_SUB, (sb + 1) * MOE_SUB)

    def xs_copy(item, sb):
        src0 = pl.multiple_of(row0_ref[item] + sb * MOE_SUB, MOE_SUB)
        return pltpu.make_async_copy(xs_hbm.at[pl.ds(src0, MOE_SUB)],
                                     xs_ref.at[item % 2, pl.ds(sb * MOE_SUB, MOE_SUB)],
                                     xsem.at[item % 2])

    def for_item_subs(item, fn):
        for sb in range(MOE_NSUB):
            @pl.when(sb < nsub_ref[item])
            def _():
                fn(sb)

    @pl.when((s == 0) & (nsub > 0))
    def _fetch_rows():
        @pl.when(i == 0)
        def _():
            for_item_subs(i, lambda sb: xs_copy(i, sb).start())

        for_item_subs(i, lambda sb: xs_copy(i, sb).wait())

        @pl.when(i + 1 < n_items)
        def _():
            for_item_subs(i + 1, lambda sb: xs_copy(i + 1, sb).start())

    @pl.when((s < n_gu) & (nsub > 0))
    def _gate_up():
        def prep():
            wgu_bf_ref[:, 0:ft2] = wgu_a_ref[0, 0].astype(BF16)
            wgu_bf_ref[:, ft2:2 * ft2] = wgu_b_ref[0, 0].astype(BF16)

        def one(sb):
            rows = sub_rows(sb)
            x = xs_ref[slot, rows, :].astype(BF16)
            hgu = _dot(x, wgu_bf_ref[...])
            act = _swiglu_pair(hgu[:, 0:ft2] + bgu_a_ref[0, 0], hgu[:, ft2:2 * ft2] + bgu_b_ref[0, 0])
            act_ref[s, rows, :] = act.astype(BF16)

        _for_each_sub(nsub, prep, one)

    def out_copy(osl, n, sb):
        dst0 = pl.multiple_of(row0 + sb * MOE_SUB, MOE_SUB)
        return pltpu.make_async_copy(
            ot_ref.at[osl, pl.ds(sb * MOE_SUB, MOE_SUB)],
            y_hbm.at[pl.ds(dst0, MOE_SUB), pl.ds(pl.multiple_of(n * MOE_TILE, MOE_TILE), MOE_TILE)],
            osem.at[osl])

    @pl.when((s >= n_gu) & (nsub > 0))
    def _down():
        n = s - n_gu
        osl = n % 2

        @pl.when(n >= 2)
        def _():
            for_item_subs(i, lambda sb: out_copy(osl, 0, sb).wait())

        half = wdn_ref.shape[2] // 2

        def prep():
            for c in range(MOE_TILE // LANES):
                cols = slice(c * LANES, (c + 1) * LANES)
                il_ref[c, pl.ds(0, half, stride=2), :] = wdn_ref[0, 0, 0:half, cols]
                il_ref[c, pl.ds(1, half, stride=2), :] = wdn_ref[0, 0, half:2 * half, cols]
                wdn_bf_ref[:, cols] = il_ref[c].astype(BF16)

        def one(sb):
            rows = sub_rows(sb)
            acc = None
            for f in range(n_gu):
                term = _dot(act_ref[f, rows, :], wdn_bf_ref[f * MOE_TILE:(f + 1) * MOE_TILE, :])
                acc = term if acc is None else acc + term
            ot_ref[osl, rows, :] = acc + bdn_ref[0, 0]

        _for_each_sub(nsub, prep, one)
        for_item_subs(i, lambda sb: out_copy(osl, n, sb).start())

        @pl.when(s == n_steps - 1)
        def _():
            for_item_subs(i, lambda sb: out_copy(1 - osl, 0, sb).wait())
            for_item_subs(i, lambda sb: out_copy(osl, 0, sb).wait())

def _moe_experts(xs, plan, w_gu, b_gu, w_dn, b_dn, layer):
    n_slots, d = xs.shape
    _, n_e, _, two_f = w_gu.shape
    d_ff = two_f // 2
    item_e, item_row0, item_nsub = plan
    n_items = item_e.shape[0]
    n_gu = (d_ff // 2) // MOE_FT
    n_dn = d // MOE_TILE
    ft2 = 2 * MOE_FT
    assert w_dn.shape[2] == n_gu * MOE_TILE

    def gu_tile(i, s, ns):
        return jnp.where(ns[i] > 0, jnp.minimum(s, n_gu - 1), n_gu - 1)

    def dn_tile(i, s, ns):
        return jnp.where(ns[i] > 0, jnp.maximum(s - n_gu, 0), n_dn - 1)

    def gu_map(half):
        return lambda i, s, ie, r0, ns: (layer, ie[i], 0, half * n_gu + gu_tile(i, s, ns))

    dn_map = lambda i, s, ie, r0, ns: (layer, ie[i], 0, dn_tile(i, s, ns))

    grid_spec = pltpu.PrefetchScalarGridSpec(
        num_scalar_prefetch=3,
        grid=(jnp.sum((item_nsub > 0).astype(jnp.int32)), n_gu + n_dn),
        in_specs=[
            pl.BlockSpec(memory_space=pl.ANY),
            pl.BlockSpec((1, 1, d, ft2), gu_map(0)),
            pl.BlockSpec((1, 1, d, ft2), gu_map(1)),
            pl.BlockSpec((1, 1, 1, ft2), gu_map(0)),
            pl.BlockSpec((1, 1, 1, ft2), gu_map(1)),
            pl.BlockSpec((1, 1, d_ff, MOE_TILE), dn_map),
            pl.BlockSpec((1, 1, 1, MOE_TILE), dn_map),
        ],
        out_specs=pl.BlockSpec(memory_space=pl.ANY),
        scratch_shapes=[
            pltpu.VMEM((2, MOE_ITEM_ROWS, d), F32),
            pltpu.VMEM((n_gu, MOE_ITEM_ROWS, MOE_TILE), BF16),
            pltpu.VMEM((2, MOE_ITEM_ROWS, MOE_TILE), F32),
            pltpu.VMEM((d, 2 * ft2), BF16),
            pltpu.VMEM((MOE_TILE // LANES, d_ff, LANES), F32),
            pltpu.VMEM((d_ff, MOE_TILE), BF16),
            pltpu.SemaphoreType.DMA((2,)),
            pltpu.SemaphoreType.DMA((2,)),
        ],
    )
    depth = w_gu.shape[0]
    return pl.pallas_call(
        functools.partial(_moe_kernel, n_gu=n_gu),
        grid_spec=grid_spec,
        out_shape=jax.ShapeDtypeStruct((n_slots, d), F32),
        input_output_aliases={3: 0},
        compiler_params=_cparams(("arbitrary", "arbitrary")),
        name="moe_experts",
    )(item_e, item_row0, item_nsub, xs, w_gu, w_gu,
      b_gu.reshape(depth, n_e, 1, two_f), b_gu.reshape(depth, n_e, 1, two_f),
      w_dn, b_dn.reshape(depth, n_e, 1, d))


def _moe_combine_kernel(pos_ref, x_ref, g_ref, gate_ref, y_hbm, o_ref, ybuf_ref, sem):
    i = pl.program_id(0)
    n = pl.num_programs(0)
    tt = x_ref.shape[0]

    def issue(step):
        sl = step % 2

        def body(u, carry):
            for q in range(DMA_UNROLL // TOP_K):
                t = u * (DMA_UNROLL // TOP_K) + q
                for k in range(TOP_K):
                    p = pos_ref[(step * tt + t) * TOP_K + k]
                    pltpu.make_async_copy(y_hbm.at[p], ybuf_ref.at[sl, k, t], sem.at[sl]).start()
            return carry

        lax.fori_loop(0, tt * TOP_K // DMA_UNROLL, body, 0)

    @pl.when(i == 0)
    def _():
        issue(i)

    @pl.when(i + 1 < n)
    def _():
        issue(i + 1)

    slot = i % 2
    pltpu.make_async_copy(ybuf_ref.at[slot], ybuf_ref.at[slot], sem.at[slot]).wait()
    gates = g_ref[...]
    tot = None
    for k in range(TOP_K):
        term = gates[:, k:k + 1] * ybuf_ref[slot, k]
        tot = term if tot is None else tot + term
    o_ref[...] = x_ref[...] + gate_ref[0] * tot


def _moe_combine(x, y, pos, gates, gate_mod):
    b, s, d = x.shape
    tt = COMBINE_TOKENS
    n_tok = b * s
    grid_spec = pltpu.PrefetchScalarGridSpec(
        num_scalar_prefetch=1,
        grid=(n_tok // tt,),
        in_specs=[
            pl.BlockSpec((tt, d), lambda i, pos_r: (i, 0)),
            pl.BlockSpec((tt, TOP_K), lambda i, pos_r: (i, 0)),
            pl.BlockSpec((1, 1, d), lambda i, pos_r: (i * tt // s, 0, 0)),
            pl.BlockSpec(memory_space=pl.ANY),
        ],
        out_specs=pl.BlockSpec((tt, d), lambda i, pos_r: (i, 0)),
        scratch_shapes=[pltpu.VMEM((2, TOP_K, tt, d), F32), pltpu.SemaphoreType.DMA((2,))],
    )
    out = pl.pallas_call(
        _moe_combine_kernel,
        grid_spec=grid_spec,
        out_shape=jax.ShapeDtypeStruct((n_tok, d), F32),
        compiler_params=_cparams(("arbitrary",)),
        name="moe_combine",
    )(pos, x.reshape(n_tok, d), gates.reshape(n_tok, TOP_K), gate_mod, y)
    return out.reshape(b, s, d)


def _moe_ffn(x, g, scale, shift, gate_mod, rw, rb, w_gu, b_gu, w_dn, b_dn, layer):
    b, s, d = x.shape
    h, top_idx, gates = _norm_router(x, g, scale, shift, rw, rb)
    item_e, item_row0, item_nsub, slot_tok, pos = _moe_plan(top_idx.reshape(b * s, TOP_K), b * s)
    xs = _moe_dispatch(h.reshape(b * s, d), slot_tok)
    y = _moe_experts(xs, (item_e, item_row0, item_nsub), w_gu, b_gu, w_dn, b_dn, layer)
    return _moe_combine(x, y, pos, gates, gate_mod)


def _rope_lane_tables(seq):
    inv_freq = ROPE_THETA ** (-jnp.arange(0, HEAD_DIM, 2, dtype=F32) / HEAD_DIM)
    ang = jnp.arange(seq, dtype=F32)[:, None] * inv_freq[None, :]
    cos, sin = jnp.cos(ang), jnp.sin(ang)
    reps = LANES // HEAD_DIM
    return (jnp.tile(jnp.concatenate([cos, cos], axis=-1), (1, reps)),
            jnp.tile(jnp.concatenate([-sin, sin], axis=-1), (1, reps)))


def _col_modes(kinds, tn, width):
    modes = []
    for n_cols, mode in kinds:
        assert n_cols % tn == 0
        modes += [mode] * (n_cols // tn)
    assert len(modes) * tn == width
    return jnp.asarray(modes, jnp.int32)


PLAIN, ROPE, SCALE, ROPE_SCALE = 0, 1, 2, 3


def kernel(x, c, ada_w, ada_b, norm_g, final_g, ab_w_in, ab_w_out, na_rpb, diff_lam_q1, diff_lam_k1,
           diff_lam_q2, diff_lam_k2, diff_subln_g, dil_w_in, dil_w_out, moe_router_w, moe_router_b,
           moe_w_gate_up, moe_b_gate_up, moe_w_down, moe_b_down):
    b, s, d = x.shape
    depth = ada_w.shape[0]
    tn = 512
    cos_t, sin_t = _rope_lane_tables(s)

    c_pad = jnp.zeros((8, d), F32).at[:b].set(c)
    mod = _adaln(c_pad, ada_w.reshape(depth * 2, d, 3 * d), ada_b.reshape(depth * 2, 3 * d))
    mod = mod[:, :b].reshape(depth, 2, b, 3, 1, d)

    na_w = NA_HEADS * HEAD_DIM
    df_w = DIFF_HEADS * 2 * HEAD_DIM
    dil_w = DIL_HEADS * HEAD_DIM

    for l in range(depth):
        shift, scale, gate = (mod[l, 0, :, t] for t in range(3))
        h = _norm_mod(x, norm_g[l, 0], scale, shift, BF16)
        i = l // 2
        if l % 2 == 0:
            lambda_init = 0.8 - 0.6 * math.exp(-0.3 * l)
            modes = _col_modes([(na_w, SCALE), (2 * na_w, PLAIN), (df_w, ROPE_SCALE), (df_w, ROPE),
                                (df_w, PLAIN)], tn, ab_w_in.shape[2])
            proj = _proj(h, ab_w_in[i], modes, cos_t, sin_t, ab_w_in.shape[2], tn=tn)
            proj = proj.reshape(b, s, ab_w_in.shape[2])
            o_na = _na_attention(proj, _na_bias_table(na_rpb[i], s // GRID_W))
            o_df = _diff_attention(proj, diff_lam_q1[i], diff_lam_k1[i], diff_lam_q2[i], diff_lam_k2[i],
                                   diff_subln_g[i], lambda_init, col0=3 * na_w // LANES)
            o = jnp.concatenate([o_na, o_df], axis=-1)
            x = _outproj(o, ab_w_out[i], x, gate)
        else:
            modes = _col_modes([(dil_w, ROPE_SCALE), (dil_w, ROPE), (dil_w, PLAIN)] * len(DIL_CONFIGS),
                               tn, dil_w_in.shape[2])
            outs, lses = [], []
            for gi, (window, dil) in enumerate(DIL_CONFIGS):
                proj = _proj(h, dil_w_in[i], modes, cos_t, sin_t, 3 * dil_w, jbase=gi * (3 * dil_w // tn),
                             dil=dil, tn=tn)
                o_g, lse_g = _dil_attention(proj, window, dil)
                outs.append(o_g)
                lses.append(lse_g)
            o = _dil_mix(outs, lses, [dil for _, dil in DIL_CONFIGS])
            x = _outproj(o, dil_w_out[i], x, gate)
        shift, scale, gate = (mod[l, 1, :, t] for t in range(3))
        x = _moe_ffn(x, norm_g[l, 1], scale, shift, gate, moe_router_w[l], moe_router_b[l],
                     moe_w_gate_up, moe_b_gate_up, moe_w_down, moe_b_down, l)
    return _final_norm(x, final_g)
```

```python
import functools
import math

import jax
import jax.numpy as jnp
from jax import lax
from jax.experimental import pallas as pl
from jax.experimental.pallas import tpu as pltpu

F32 = jnp.float32
BF16 = jnp.bfloat16

HEAD_DIM = 64
LANES = 128
GRID_W = 64
NA_HEADS = 16
NA_ROWS = 8
NA_COLS = 16
DIFF_HEADS = 8
DIL_HEADS = 32
DIL_CONFIGS = ((128, 1), (512, 4), (2048, 16))
ROPE_THETA = 10000.0
N_EXPERTS = 32
TOP_K = 4
SWIGLU_LIMIT = 7.0
SWIGLU_ALPHA = 1.702
RMS_EPS = 1e-5

PROJ_MCHUNKS = 4
DIL_QBLOCK = 128
DIL_BLOCK_UNROLL = 4
NA_ROW_UNROLL = 4
MOE_SUB = 256
MOE_NSUB = 5
MOE_ITEM_ROWS = MOE_NSUB * MOE_SUB
MOE_FT = 128
MOE_TILE = 2 * MOE_FT
PROJ_TN = 1024
DMA_UNROLL = 8
COMBINE_TOKENS = 128
VMEM_LIMIT = 56 * 1024 * 1024


def _cparams(sem, vmem=VMEM_LIMIT):
    return pltpu.CompilerParams(dimension_semantics=sem, vmem_limit_bytes=vmem)


def _dot(a, b):
    return jnp.dot(a, b, preferred_element_type=F32)


def _dot_nt(a, b):
    return lax.dot_general(a, b, (((1,), (1,)), ((), ())), preferred_element_type=F32)


def _split_bf16(a):
    hi = a.astype(BF16)
    lo = (a - hi.astype(F32)).astype(BF16)
    return hi, lo


def _dot3(a, b):
    ah, al = _split_bf16(a)
    bh, bl = _split_bf16(b)
    return _dot(ah, bh) + (_dot(ah, bl) + _dot(al, bh))


def _adaln_kernel(c_ref, w_ref, b_ref, o_ref):
    c = c_ref[...]
    c_act = c * jax.nn.sigmoid(c)
    o_ref[0] = _dot3(c_act, w_ref[0]) + b_ref[0]


def _adaln(c_pad, w, b, tn=768):
    n_mod, d, n = w.shape
    return pl.pallas_call(
        _adaln_kernel,
        grid=(n_mod, n // tn),
        in_specs=[
            pl.BlockSpec(c_pad.shape, lambda l, j: (0, 0)),
            pl.BlockSpec((1, d, tn), lambda l, j: (l, 0, j)),
            pl.BlockSpec((1, 1, tn), lambda l, j: (l, 0, j)),
        ],
        out_specs=pl.BlockSpec((1, c_pad.shape[0], tn), lambda l, j: (l, 0, j)),
        out_shape=jax.ShapeDtypeStruct((n_mod, c_pad.shape[0], n), F32),
        compiler_params=_cparams(("arbitrary", "arbitrary")),
        name="adaln",
    )(c_pad, w, b.reshape(n_mod, 1, n))


def _rms(x, g):
    return x * lax.rsqrt(jnp.mean(x * x, axis=-1, keepdims=True) + RMS_EPS) * g


def _norm_mod_kernel(x_ref, g_ref, scale_ref, shift_ref, o_ref):
    h = _rms(x_ref[0], g_ref[...]) * (1.0 + scale_ref[0]) + shift_ref[0]
    o_ref[0] = h.astype(o_ref.dtype)


def _norm_mod(x, g, scale, shift, out_dtype, tm=512):
    b, s, d = x.shape
    return pl.pallas_call(
        _norm_mod_kernel,
        grid=(b, s // tm),
        in_specs=[
            pl.BlockSpec((1, tm, d), lambda bi, i: (bi, i, 0)),
            pl.BlockSpec((1, d), lambda bi, i: (0, 0)),
            pl.BlockSpec((1, 1, d), lambda bi, i: (bi, 0, 0)),
            pl.BlockSpec((1, 1, d), lambda bi, i: (bi, 0, 0)),
        ],
        out_specs=pl.BlockSpec((1, tm, d), lambda bi, i: (bi, i, 0)),
        out_shape=jax.ShapeDtypeStruct((b, s, d), out_dtype),
        compiler_params=_cparams(("arbitrary", "arbitrary")),
        name="norm_mod",
    )(x, g.reshape(1, d), scale, shift)


def _final_norm_kernel(x_ref, g_ref, o_ref):
    o_ref[0] = _rms(x_ref[0], g_ref[...])


def _final_norm(x, g, tm=512):
    b, s, d = x.shape
    return pl.pallas_call(
        _final_norm_kernel,
        grid=(b, s // tm),
        in_specs=[
            pl.BlockSpec((1, tm, d), lambda bi, i: (bi, i, 0)),
            pl.BlockSpec((1, d), lambda bi, i: (0, 0)),
        ],
        out_specs=pl.BlockSpec((1, tm, d), lambda bi, i: (bi, i, 0)),
        out_shape=jax.ShapeDtypeStruct((b, s, d), F32),
        compiler_params=_cparams(("arbitrary", "arbitrary")),
        name="final_norm",
    )(x, g.reshape(1, d))


def _proj_kernel(mode_ref, a_ref, w_ref, cos_ref, sin_ref, o_ref, wbf_ref, stage_ref, *, jbase, dil):
    j = pl.program_id(0)

    @pl.when((pl.program_id(1) == 0) & (pl.program_id(2) == 0))
    def _():
        wbf_ref[...] = w_ref[...].astype(BF16)

    mode = mode_ref[jbase + j]
    tm = a_ref.shape[1]
    tn = wbf_ref.shape[1]
    cm = tm // PROJ_MCHUNKS

    def emit(mi, ci, val):
        cols = slice(ci * LANES, (ci + 1) * LANES)
        if dil == 1:
            o_ref[0, 0, mi * cm:(mi + 1) * cm, cols] = val.astype(o_ref.dtype)
        else:
            stage_ref[ci, mi * cm:(mi + 1) * cm, :] = val
            n = cm // dil
            for r in range(dil):
                o_ref[0, r, mi * n:(mi + 1) * n, cols] = (
                    stage_ref[ci, pl.ds(mi * cm + r, n, stride=dil), :].astype(o_ref.dtype))

    @pl.when(mode == 0)
    def _():
        for mi in range(PROJ_MCHUNKS):
            acc = _dot(a_ref[0, mi * cm:(mi + 1) * cm, :], wbf_ref[...])
            for ci in range(tn // LANES):
                emit(mi, ci, acc[:, ci * LANES:(ci + 1) * LANES])

    @pl.when(mode != 0)
    def _():
        rope_on = (mode & 1).astype(F32)
        qscale = jnp.where((mode & 2) == 2, HEAD_DIM ** -0.5, 1.0).astype(F32)
        lane = lax.broadcasted_iota(jnp.int32, (1, LANES), 1)
        first_half = (lane % HEAD_DIM) < (HEAD_DIM // 2)
        for mi in range(PROJ_MCHUNKS):
            rows = slice(mi * cm, (mi + 1) * cm)
            acc = _dot(a_ref[0, rows, :], wbf_ref[...])
            cos = (cos_ref[rows, :] * rope_on + (1.0 - rope_on)) * qscale
            sin = sin_ref[rows, :] * (rope_on * qscale)
            for ci in range(tn // LANES):
                x = acc[:, ci * LANES:(ci + 1) * LANES]
                partner = jnp.where(first_half,
                                    pltpu.roll(x, LANES - HEAD_DIM // 2, 1),
                                    pltpu.roll(x, HEAD_DIM // 2, 1))
                emit(mi, ci, x * cos + partner * sin)


def _proj(a, w, modes, cos_t, sin_t, n_out, jbase=0, dil=1, tm=1024, tn=PROJ_TN):
    b, s, k = a.shape
    grid_spec = pltpu.PrefetchScalarGridSpec(
        num_scalar_prefetch=1,
        grid=(n_out // tn, b, s // tm),
        in_specs=[
            pl.BlockSpec((1, tm, k), lambda j, bi, i, m: (bi, i, 0)),
            pl.BlockSpec((k, tn), lambda j, bi, i, m: (0, jbase + j)),
            pl.BlockSpec((tm, LANES), lambda j, bi, i, m: (i, 0)),
            pl.BlockSpec((tm, LANES), lambda j, bi, i, m: (i, 0)),
        ],
        out_specs=pl.BlockSpec((1, dil, tm // dil, tn), lambda j, bi, i, m: (bi, 0, i, j)),
        scratch_shapes=[pltpu.VMEM((k, tn), BF16), pltpu.VMEM((tn // LANES, tm, LANES), F32)],
    )
    return pl.pallas_call(
        functools.partial(_proj_kernel, jbase=jbase, dil=dil),
        grid_spec=grid_spec,
        out_shape=jax.ShapeDtypeStruct((b, dil, s // dil, n_out), BF16),
        compiler_params=_cparams(("arbitrary", "arbitrary", "arbitrary")),
        name="proj",
    )(modes, a, w, cos_t, sin_t)


def _outproj_kernel(a_ref, w_ref, x_ref, gate_ref, o_ref, wbf_ref):
    @pl.when((pl.program_id(1) == 0) & (pl.program_id(2) == 0))
    def _():
        wbf_ref[...] = w_ref[...].astype(BF16)

    o_ref[0] = x_ref[0] + gate_ref[0] * _dot(a_ref[0], wbf_ref[...])


def _outproj(a, w, x, gate, tm=1024, tn=512):
    b, s, k = a.shape
    n = w.shape[1]
    return pl.pallas_call(
        _outproj_kernel,
        grid=(n // tn, b, s // tm),
        in_specs=[
            pl.BlockSpec((1, tm, k), lambda j, bi, i: (bi, i, 0)),
            pl.BlockSpec((k, tn), lambda j, bi, i: (0, j)),
            pl.BlockSpec((1, tm, tn), lambda j, bi, i: (bi, i, j)),
            pl.BlockSpec((1, 1, tn), lambda j, bi, i: (bi, 0, j)),
        ],
        out_specs=pl.BlockSpec((1, tm, tn), lambda j, bi, i: (bi, i, j)),
        out_shape=jax.ShapeDtypeStruct((b, s, n), F32),
        scratch_shapes=[pltpu.VMEM((k, tn), BF16)],
        compiler_params=_cparams(("arbitrary", "arbitrary", "arbitrary")),
        name="outproj",
    )(a, w, x, gate)


def _columns(cols):
    m = cols[0].shape[0]
    lane = lax.broadcasted_iota(jnp.int32, (m, len(cols)), 1)
    out = jnp.broadcast_to(cols[0], (m, len(cols)))
    for k in range(1, len(cols)):
        out = jnp.where(lane == k, cols[k], out)
    return out


def _head_masks():
    lane = lax.broadcasted_iota(jnp.int32, (1, LANES), 1)
    return lane < HEAD_DIM, lane >= HEAD_DIM


def _na_kernel(q_ref, k_ref, v_ref, bias_ref, o_ref, *, rows):
    kr = min(NA_ROWS, rows)
    masks = _head_masks()

    def body(t, carry):
        r_list = [t * NA_ROW_UNROLL + u for u in range(NA_ROW_UNROLL)]
        scores, values = [], []
        for r in r_list:
            rs = jnp.clip(r - kr // 2, 0, rows - kr)
            var = r - rs
            q = q_ref[0, pl.ds(pl.multiple_of(r * GRID_W, GRID_W), GRID_W), :]
            k = k_ref[0, pl.ds(pl.multiple_of(rs * GRID_W, GRID_W), kr * GRID_W), :]
            values.append(v_ref[0, pl.ds(pl.multiple_of(rs * GRID_W, GRID_W), kr * GRID_W), :])
            for h in range(2):
                qh = jnp.where(masks[h], q, jnp.zeros_like(q))
                scores.append(_dot_nt(qh, k) + bias_ref[h, var])
        probs, dens = [], []
        for sc in scores:
            e = jnp.exp(sc - jnp.max(sc, axis=-1, keepdims=True))
            dens.append(jnp.sum(e, axis=-1, keepdims=True))
            probs.append(e.astype(BF16))
        for u, r in enumerate(r_list):
            outs = [_dot(probs[2 * u + h], values[u]) / dens[2 * u + h] for h in range(2)]
            o = jnp.where(masks[0], outs[0], outs[1])
            o_ref[0, pl.ds(pl.multiple_of(r * GRID_W, GRID_W), GRID_W), :] = o.astype(o_ref.dtype)
        return carry

    assert rows % NA_ROW_UNROLL == 0
    lax.fori_loop(0, rows // NA_ROW_UNROLL, body, 0)


def _na_bias_table(rpb, rows):
    kr = min(NA_ROWS, rows)
    c = jnp.arange(GRID_W)
    kc = jnp.arange(GRID_W)
    cs = jnp.clip(c - NA_COLS // 2, 0, GRID_W - NA_COLS)
    valid = (kc[None, :] >= cs[:, None]) & (kc[None, :] < cs[:, None] + NA_COLS)
    coff = jnp.clip(kc[None, :] - c[:, None] + (NA_COLS - 1), 0, 2 * NA_COLS - 2)
    rows_v = jnp.stack([rpb[:, NA_ROWS - 1 - v:NA_ROWS - 1 - v + kr, :] for v in range(kr)], axis=1)
    onehot = ((coff[None] == jnp.arange(2 * NA_COLS - 1)[:, None, None]) & valid[None]).astype(F32)
    tbl = jnp.einsum('hvio,ock->hvcik', rows_v.astype(F32), onehot, precision=lax.Precision.HIGHEST)
    tbl = jnp.where(valid[None, None, :, None, :], tbl, -jnp.inf)
    return tbl.reshape(rpb.shape[0], kr, GRID_W, kr * GRID_W)


def _na_attention(proj, bias_tbl):
    b, s, _ = proj.shape
    rows = s // GRID_W
    npair = NA_HEADS // 2
    kr = bias_tbl.shape[1]
    blk = lambda off: pl.BlockSpec((1, s, LANES), lambda bi, c: (bi, 0, off + c))
    return pl.pallas_call(
        functools.partial(_na_kernel, rows=rows),
        grid=(b, npair),
        in_specs=[
            blk(0), blk(npair), blk(2 * npair),
            pl.BlockSpec((2, kr, GRID_W, kr * GRID_W), lambda bi, c: (c, 0, 0, 0)),
        ],
        out_specs=pl.BlockSpec((1, s, LANES), lambda bi, c: (bi, 0, c)),
        out_shape=jax.ShapeDtypeStruct((b, s, NA_HEADS * HEAD_DIM), BF16),
        compiler_params=_cparams(("arbitrary", "arbitrary")),
        name="na_attn",
    )(proj, proj, proj, bias_tbl)


def _diff_kernel(lam_ref, q_ref, k_ref, v_ref, g_ref, o_ref, *, tk, lambda_init):
    masks = _head_masks()
    q = q_ref[0]
    tq = q.shape[0]
    s = k_ref.shape[1]
    qs = [jnp.where(masks[m], q, jnp.zeros_like(q)) for m in range(2)]

    def body(j, carry):
        k = k_ref[0, pl.ds(pl.multiple_of(j * tk, tk), tk), :]
        v = v_ref[0, pl.ds(pl.multiple_of(j * tk, tk), tk), :]
        scores = [_dot_nt(qs[m], k) for m in range(2)]
        stats = []
        for m in range(2):
            m_old, l_old = carry[3 * m], carry[3 * m + 1]
            m_new = jnp.maximum(m_old, jnp.max(scores[m], axis=-1, keepdims=True))
            alpha = jnp.exp(m_old - m_new)
            e = jnp.exp(scores[m] - m_new)
            l_new = alpha * l_old + jnp.sum(e, axis=-1, keepdims=True)
            stats.append((m_new, l_new, alpha, e.astype(BF16)))
        new = []
        for m in range(2):
            m_new, l_new, alpha, p = stats[m]
            new += [m_new, l_new, alpha * carry[3 * m + 2] + _dot(p, v)]
        return tuple(new)

    init = []
    for m in range(2):
        init += [jnp.full((tq, 1), -jnp.inf, F32), jnp.zeros((tq, 1), F32), jnp.zeros((tq, LANES), F32)]
    res = lax.fori_loop(0, s // tk, body, tuple(init))
    lam = lam_ref[0]
    o = res[2] / res[1] - lam * (res[5] / res[4])
    o_ref[0] = (_rms(o, g_ref[...]) * (1.0 - lambda_init)).astype(o_ref.dtype)


def _diff_lambda_kernel(lq1_ref, lk1_ref, lq2_ref, lk2_ref, o_ref, *, lambda_init):
    s1 = jnp.sum(lq1_ref[...] * lk1_ref[...], axis=-1, keepdims=True)
    s2 = jnp.sum(lq2_ref[...] * lk2_ref[...], axis=-1, keepdims=True)
    o_ref[...] = jnp.exp(s1) - jnp.exp(s2) + lambda_init


def _diff_attention(proj, lq1, lk1, lq2, lk2, subln_g, lambda_init, col0, tq=512, tk=512):
    b, s, _ = proj.shape
    nh = DIFF_HEADS
    vec = lambda a: a.reshape(1, HEAD_DIM).astype(F32)
    lam = pl.pallas_call(
        functools.partial(_diff_lambda_kernel, lambda_init=lambda_init),
        out_shape=jax.ShapeDtypeStruct((1, 1), F32),
        name="diff_lambda",
    )(vec(lq1), vec(lk1), vec(lq2), vec(lk2)).reshape(1)
    grid_spec = pltpu.PrefetchScalarGridSpec(
        num_scalar_prefetch=1,
        grid=(b, nh, s // tq),
        in_specs=[
            pl.BlockSpec((1, tq, LANES), lambda bi, h, i, lam_r: (bi, i, col0 + h)),
            pl.BlockSpec((1, s, LANES), lambda bi, h, i, lam_r: (bi, 0, col0 + nh + h)),
            pl.BlockSpec((1, s, LANES), lambda bi, h, i, lam_r: (bi, 0, col0 + 2 * nh + h)),
            pl.BlockSpec((1, LANES), lambda bi, h, i, lam_r: (0, 0)),
        ],
        out_specs=pl.BlockSpec((1, tq, LANES), lambda bi, h, i, lam_r: (bi, i, h)),
    )
    return pl.pallas_call(
        functools.partial(_diff_kernel, tk=tk, lambda_init=lambda_init),
        grid_spec=grid_spec,
        out_shape=jax.ShapeDtypeStruct((b, s, nh * LANES), BF16),
        compiler_params=_cparams(("arbitrary", "arbitrary", "arbitrary")),
        name="diff_attn",
    )(lam, proj, proj, proj, subln_g.reshape(1, LANES).astype(F32))


def _dil_kernel(q_ref, k_ref, v_ref, o_ref, lse_ref, *, n_side):
    masks = _head_masks()
    length = q_ref.shape[2]
    tq = DIL_QBLOCK
    tk = tq + 2 * n_side
    rel0 = (lax.broadcasted_iota(jnp.int32, (tq, tk), 1)
            - lax.broadcasted_iota(jnp.int32, (tq, tk), 0))

    nq = length // tq
    planes = q_ref.shape[1]

    def body(t, carry):
        where, scores, values = [], [], []
        for u in range(DIL_BLOCK_UNROLL):
            bi = t * DIL_BLOCK_UNROLL + u
            p = bi // nq
            a0 = pl.multiple_of((bi % nq) * tq, tq)
            ks = pl.multiple_of(jnp.clip(a0 - n_side, 0, length - tk), n_side)
            q = q_ref[0, p, pl.ds(a0, tq), :]
            k = k_ref[0, p, pl.ds(ks, tk), :]
            values.append(v_ref[0, p, pl.ds(ks, tk), :])
            where.append((p, a0))
            rel = rel0 + (ks - a0)
            band = (rel >= -n_side) & (rel <= n_side)
            for h in range(2):
                qh = jnp.where(masks[h], q, jnp.zeros_like(q))
                scores.append(jnp.where(band, _dot_nt(qh, k), -jnp.inf))
        probs, dens, lses = [], [], []
        for sc in scores:
            m = jnp.max(sc, axis=-1, keepdims=True)
            e = jnp.exp(sc - m)
            den = jnp.sum(e, axis=-1, keepdims=True)
            probs.append(e.astype(BF16))
            dens.append(den)
            lses.append(m + jnp.log(den))
        for u, (p, a0) in enumerate(where):
            outs = [_dot(probs[2 * u + h], values[u]) / dens[2 * u + h] for h in range(2)]
            o_ref[0, p, pl.ds(a0, tq), :] = jnp.where(masks[0], outs[0], outs[1]).astype(o_ref.dtype)
            lse_ref[0, 0, p, pl.ds(a0, tq), :] = _columns(lses[2 * u:2 * u + 2])
        return carry

    assert (planes * nq) % DIL_BLOCK_UNROLL == 0
    lax.fori_loop(0, planes * nq // DIL_BLOCK_UNROLL, body, 0)


def _dil_attention(proj, window, dil):
    b, _, length, _ = proj.shape
    n_side = (window // 2) // dil
    npair = DIL_HEADS // 2
    cw = DIL_HEADS * HEAD_DIM
    rp = min(dil, max(1, DIL_BLOCK_UNROLL * DIL_QBLOCK // length))
    blk = lambda off: pl.BlockSpec((1, rp, length, LANES), lambda bi, c, r: (bi, r, 0, off + c))
    return pl.pallas_call(
        functools.partial(_dil_kernel, n_side=n_side),
        grid=(b, npair, dil // rp),
        in_specs=[blk(0), blk(npair), blk(2 * npair)],
        out_specs=[
            pl.BlockSpec((1, rp, length, LANES), lambda bi, c, r: (bi, r, 0, c)),
            pl.BlockSpec((1, 1, rp, length, 2), lambda bi, c, r: (c, bi, r, 0, 0)),
        ],
        out_shape=[
            jax.ShapeDtypeStruct((b, dil, length, cw), BF16),
            jax.ShapeDtypeStruct((npair, b, dil, length, 2), F32),
        ],
        compiler_params=_cparams(("arbitrary", "arbitrary", "arbitrary")),
        name="dil_attn",
    )(proj, proj, proj)


def _dil_mix_kernel(*refs, dils):
    ng = len(dils)
    o_refs, l_refs = refs[:ng], refs[ng:2 * ng]
    out_ref, so_ref, sl_ref = refs[2 * ng:]
    masks = _head_masks()
    tt = out_ref.shape[1]

    def widen(lse):
        return jnp.where(masks[0], lse[:, 0:1], lse[:, 1:2])

    outs, lses = [], []
    for g, dil in enumerate(dils):
        if dil == 1:
            outs.append(o_refs[g][0, 0].astype(F32))
            lses.append(widen(l_refs[g][0, 0, 0]))
        else:
            n = tt // dil
            for r in range(dil):
                so_ref[g, pl.ds(r, n, stride=dil), :] = o_refs[g][0, r].astype(F32)
                sl_ref[g, pl.ds(r, n, stride=dil), :] = widen(l_refs[g][0, 0, r])
            outs.append(so_ref[g])
            lses.append(sl_ref[g])
    mx = lses[0]
    for l in lses[1:]:
        mx = jnp.maximum(mx, l)
    ws = [jnp.exp(l - mx) for l in lses]
    tot = ws[0]
    for w in ws[1:]:
        tot = tot + w
    acc = None
    for w, o in zip(ws, outs):
        term = (w / tot) * o
        acc = term if acc is None else acc + term
    out_ref[0] = acc.astype(out_ref.dtype)


def _dil_mix(outs, lses, dils, tt=1024):
    b, _, _, cw = outs[0].shape
    s = outs[0].shape[1] * outs[0].shape[2]
    npair = cw // LANES
    ng = len(dils)
    o_specs = [pl.BlockSpec((1, d, tt // d, LANES), lambda bi, i, c: (bi, 0, i, c)) for d in dils]
    l_specs = [pl.BlockSpec((1, 1, d, tt // d, 2), lambda bi, i, c: (c, bi, 0, i, 0)) for d in dils]
    return pl.pallas_call(
        functools.partial(_dil_mix_kernel, dils=tuple(dils)),
        grid=(b, s // tt, npair),
        in_specs=o_specs + l_specs,
        out_specs=pl.BlockSpec((1, tt, LANES), lambda bi, i, c: (bi, i, c)),
        out_shape=jax.ShapeDtypeStruct((b, s, cw), BF16),
        scratch_shapes=[pltpu.VMEM((ng, tt, LANES), F32), pltpu.VMEM((ng, tt, LANES), F32)],
        compiler_params=_cparams(("arbitrary", "arbitrary", "arbitrary")),
        name="dil_mix",
    )(*outs, *lses)


def _norm_router_kernel(x_ref, g_ref, scale_ref, shift_ref, rw_ref, rb_ref, h_ref, idx_ref, gate_ref):
    h = _rms(x_ref[0], g_ref[...]) * (1.0 + scale_ref[0]) + shift_ref[0]
    h_ref[0] = h
    logits = _dot3(h, rw_ref[...]) + rb_ref[...]
    n_e = logits.shape[1]
    eidx = lax.broadcasted_iota(jnp.int32, logits.shape, 1).astype(F32)
    vals, idxs = [], []
    for _ in range(TOP_K):
        m = jnp.max(logits, axis=-1, keepdims=True)
        first = jnp.min(jnp.where(logits == m, eidx, float(n_e)), axis=-1, keepdims=True)
        vals.append(m)
        idxs.append(first)
        logits = jnp.where(eidx == first, -jnp.inf, logits)
    top = _columns(vals)
    e = jnp.exp(top - vals[0])
    gate_ref[0] = e / jnp.sum(e, axis=-1, keepdims=True)
    idx_ref[0] = _columns(idxs).astype(jnp.int32)


def _norm_router(x, g, scale, shift, rw, rb, tm=512):
    b, s, d = x.shape
    n_e = rw.shape[1]
    tok = lambda w: pl.BlockSpec((1, tm, w), lambda bi, i: (bi, i, 0))
    return pl.pallas_call(
        _norm_router_kernel,
        grid=(b, s // tm),
        in_specs=[
            tok(d),
            pl.BlockSpec((1, d), lambda bi, i: (0, 0)),
            pl.BlockSpec((1, 1, d), lambda bi, i: (bi, 0, 0)),
            pl.BlockSpec((1, 1, d), lambda bi, i: (bi, 0, 0)),
            pl.BlockSpec((d, n_e), lambda bi, i: (0, 0)),
            pl.BlockSpec((1, n_e), lambda bi, i: (0, 0)),
        ],
        out_specs=[tok(d), tok(TOP_K), tok(TOP_K)],
        out_shape=[
            jax.ShapeDtypeStruct((b, s, d), F32),
            jax.ShapeDtypeStruct((b, s, TOP_K), jnp.int32),
            jax.ShapeDtypeStruct((b, s, TOP_K), F32),
        ],
        compiler_params=_cparams(("arbitrary", "arbitrary")),
        name="norm_router",
    )(x, g.reshape(1, d), scale, shift, rw, rb.reshape(1, n_e))


def _moe_plan(top_idx, n_tok):
    n_assign = n_tok * TOP_K
    flat_e = top_idx.reshape(-1)
    onehot = (flat_e[:, None] == jnp.arange(N_EXPERTS, dtype=jnp.int32)[None, :]).astype(jnp.int32)
    csum = jnp.cumsum(onehot, axis=0)
    counts = csum[-1]
    rank = jnp.take_along_axis(csum, flat_e[:, None], axis=1)[:, 0] - 1
    padded = (counts + MOE_SUB - 1) // MOE_SUB * MOE_SUB
    pstarts = jnp.cumsum(padded) - padded
    dest = pstarts[flat_e] + rank
    n_slots = n_assign + N_EXPERTS * MOE_SUB
    slot_tok = jnp.zeros((n_slots,), jnp.int32).at[dest].set(
        jnp.arange(n_assign, dtype=jnp.int32) // TOP_K)
    n_items = (n_slots + N_EXPERTS * (MOE_ITEM_ROWS - MOE_SUB)) // MOE_ITEM_ROWS
    items_per_e = (padded + MOE_ITEM_ROWS - 1) // MOE_ITEM_ROWS
    item_ends = jnp.cumsum(items_per_e)
    total_items = item_ends[-1]
    i = jnp.arange(n_items, dtype=jnp.int32)
    i_eff = jnp.minimum(i, total_items - 1)
    e_i = jnp.clip(jnp.searchsorted(item_ends, i_eff, side='right'), 0, N_EXPERTS - 1).astype(jnp.int32)
    local = i_eff - (item_ends[e_i] - items_per_e[e_i])
    row0 = pstarts[e_i] + local * MOE_ITEM_ROWS
    nrows = jnp.clip(padded[e_i] - local * MOE_ITEM_ROWS, 0, MOE_ITEM_ROWS)
    nsub = jnp.where(i < total_items, nrows // MOE_SUB, 0)
    return (e_i.astype(jnp.int32), row0.astype(jnp.int32), nsub.astype(jnp.int32),
            slot_tok, dest.astype(jnp.int32))


def _swiglu_pair(h_a, h_b):
    n = h_a.shape[1]
    lane = lax.broadcasted_iota(jnp.int32, (1, n), 1)
    even = (lane % 2) == 0

    def glu_lin(h):
        glu = jnp.minimum(h, SWIGLU_LIMIT)
        glu = glu * jax.nn.sigmoid(SWIGLU_ALPHA * glu)
        lin = jnp.clip(h, -SWIGLU_LIMIT, SWIGLU_LIMIT) + 1.0
        return glu, lin

    glu_a, lin_a = glu_lin(h_a)
    glu_b, lin_b = glu_lin(h_b)
    act_a = glu_a * pltpu.roll(lin_a, n - 1, 1)
    act_b = pltpu.roll(glu_b, 1, 1) * lin_b
    return jnp.where(even, act_a, act_b)


def _for_each_sub(nsub, prep, fn):
    for count in range(1, MOE_NSUB + 1):
        @pl.when(nsub == count)
        def _():
            prep()
            for sb in range(count):
                fn(sb)


def _moe_kernel(ie_ref, row0_ref, nsub_ref, stok_ref,
                h_hbm, y_init_hbm, wgu_a_ref, wgu_b_ref, bgu_a_ref, bgu_b_ref, wdn_ref, bdn_ref,
                y_hbm,
                xs_ref, act_ref, ot_ref, wgu_bf_ref, il_ref, wdn_bf_ref, xsem, osem, *, n_gu, n_steps):
    del y_init_hbm
    i = pl.program_id(0)
    s = pl.program_id(1)
    n_items = pl.num_programs(0)
    nsub = nsub_ref[i]
    row0 = row0_ref[i]
    slot = i % 2
    ft2 = 2 * MOE_FT
    rows_per_step = MOE_ITEM_ROWS // n_steps

    def sub_rows(sb):
        return slice(sb * MOE_SUB, (sb + 1) * MOE_SUB)

    def gather_start(item, sl, r):
        tok = stok_ref[row0_ref[item] + r]
        pltpu.make_async_copy(h_hbm.at[tok], xs_ref.at[sl, r], xsem.at[sl]).start()

    def gather_wait(sl):
        pltpu.make_async_copy(xs_ref.at[sl], xs_ref.at[sl], xsem.at[sl]).wait()

    def for_item_subs(item, fn):
        for sb in range(MOE_NSUB):
            @pl.when(sb < nsub_ref[item])
            def _():
                fn(sb)

    @pl.when(s == 0)
    def _own_rows():
        @pl.when(i == 0)
        def _():
            def body(u, carry):
                for q in range(DMA_UNROLL):
                    gather_start(i, slot, u * DMA_UNROLL + q)
                return carry

            lax.fori_loop(0, MOE_ITEM_ROWS // DMA_UNROLL, body, 0)

        gather_wait(slot)

    nxt = jnp.minimum(i + 1, n_items - 1)

    def prefetch_share():
        for q in range(rows_per_step):
            gather_start(nxt, 1 - slot, s * rows_per_step + q)

    @pl.when(s < n_gu)
    def _gate_up():
        def prep():
            prefetch_share()
            wgu_bf_ref[:, 0:ft2] = wgu_a_ref[0, 0].astype(BF16)
            wgu_bf_ref[:, ft2:2 * ft2] = wgu_b_ref[0, 0].astype(BF16)

        def one(sb):
            rows = sub_rows(sb)
            x = xs_ref[slot, rows, :].astype(BF16)
            hgu = _dot(x, wgu_bf_ref[...])
            act = _swiglu_pair(hgu[:, 0:ft2] + bgu_a_ref[0, 0], hgu[:, ft2:2 * ft2] + bgu_b_ref[0, 0])
            act_ref[s, rows, :] = act.astype(BF16)

        _for_each_sub(nsub, prep, one)

    def out_copy(osl, n, sb):
        dst0 = pl.multiple_of(row0 + sb * MOE_SUB, MOE_SUB)
        return pltpu.make_async_copy(
            ot_ref.at[osl, pl.ds(sb * MOE_SUB, MOE_SUB)],
            y_hbm.at[pl.ds(dst0, MOE_SUB), pl.ds(pl.multiple_of(n * MOE_TILE, MOE_TILE), MOE_TILE)],
            osem.at[osl])

    @pl.when(s >= n_gu)
    def _down():
        n = s - n_gu
        osl = n % 2

        @pl.when(n >= 2)
        def _():
            for_item_subs(i, lambda sb: out_copy(osl, 0, sb).wait())

        half = wdn_ref.shape[2] // 2

        def prep():
            prefetch_share()
            for c in range(MOE_TILE // LANES):
                cols = slice(c * LANES, (c + 1) * LANES)
                il_ref[c, pl.ds(0, half, stride=2), :] = wdn_ref[0, 0, 0:half, cols]
                il_ref[c, pl.ds(1, half, stride=2), :] = wdn_ref[0, 0, half:2 * half, cols]
                wdn_bf_ref[:, cols] = il_ref[c].astype(BF16)

        def one(sb):
            rows = sub_rows(sb)
            acc = None
            for f in range(n_gu):
                term = _dot(act_ref[f, rows, :], wdn_bf_ref[f * MOE_TILE:(f + 1) * MOE_TILE, :])
                acc = term if acc is None else acc + term
            ot_ref[osl, rows, :] = acc + bdn_ref[0, 0]

        _for_each_sub(nsub, prep, one)
        for_item_subs(i, lambda sb: out_copy(osl, n, sb).start())

        @pl.when(s == n_steps - 1)
        def _():
            for_item_subs(i, lambda sb: out_copy(1 - osl, 0, sb).wait())
            for_item_subs(i, lambda sb: out_copy(osl, 0, sb).wait())

            @pl.when(i == n_items - 1)
            def _():
                gather_wait(1 - slot)


def _moe_experts(h, y_init, plan, w_gu, b_gu, w_dn, b_dn, layer):
    n_slots, d = y_init.shape
    _, n_e, _, two_f = w_gu.shape
    d_ff = two_f // 2
    item_e, item_row0, item_nsub, slot_tok = plan
    n_gu = (d_ff // 2) // MOE_FT
    n_dn = d // MOE_TILE
    n_steps = n_gu + n_dn
    ft2 = 2 * MOE_FT
    assert w_dn.shape[2] == n_gu * MOE_TILE and MOE_ITEM_ROWS % n_steps == 0
    slot_tok = jnp.concatenate([slot_tok, jnp.zeros((MOE_ITEM_ROWS,), jnp.int32)])

    def gu_map(half):
        return lambda i, s, ie, r0, ns, st: (layer, ie[i], 0, half * n_gu + jnp.minimum(s, n_gu - 1))

    dn_map = lambda i, s, ie, r0, ns, st: (layer, ie[i], 0, jnp.maximum(s - n_gu, 0))

    grid_spec = pltpu.PrefetchScalarGridSpec(
        num_scalar_prefetch=4,
        grid=(jnp.sum((item_nsub > 0).astype(jnp.int32)), n_steps),
        in_specs=[
            pl.BlockSpec(memory_space=pl.ANY),
            pl.BlockSpec(memory_space=pl.ANY),
            pl.BlockSpec((1, 1, d, ft2), gu_map(0)),
            pl.BlockSpec((1, 1, d, ft2), gu_map(1)),
            pl.BlockSpec((1, 1, 1, ft2), gu_map(0)),
            pl.BlockSpec((1, 1, 1, ft2), gu_map(1)),
            pl.BlockSpec((1, 1, d_ff, MOE_TILE), dn_map),
            pl.BlockSpec((1, 1, 1, MOE_TILE), dn_map),
        ],
        out_specs=pl.BlockSpec(memory_space=pl.ANY),
        scratch_shapes=[
            pltpu.VMEM((2, MOE_ITEM_ROWS, d), F32),
            pltpu.VMEM((n_gu, MOE_ITEM_ROWS, MOE_TILE), BF16),
            pltpu.VMEM((2, MOE_ITEM_ROWS, MOE_TILE), F32),
            pltpu.VMEM((d, 2 * ft2), BF16),
            pltpu.VMEM((MOE_TILE // LANES, d_ff, LANES), F32),
            pltpu.VMEM((d_ff, MOE_TILE), BF16),
            pltpu.SemaphoreType.DMA((2,)),
            pltpu.SemaphoreType.DMA((2,)),
        ],
    )
    depth = w_gu.shape[0]
    return pl.pallas_call(
        functools.partial(_moe_kernel, n_gu=n_gu, n_steps=n_steps),
        grid_spec=grid_spec,
        out_shape=jax.ShapeDtypeStruct((n_slots, d), F32),
        input_output_aliases={5: 0},
        compiler_params=_cparams(("arbitrary", "arbitrary")),
        name="moe_experts",
    )(item_e, item_row0, item_nsub, slot_tok, h, y_init, w_gu, w_gu,
      b_gu.reshape(depth, n_e, 1, two_f), b_gu.reshape(depth, n_e, 1, two_f),
      w_dn, b_dn.reshape(depth, n_e, 1, d))


def _moe_combine_kernel(pos_ref, x_ref, g_ref, gate_ref, y_hbm, o_ref, ybuf_ref, sem):
    i = pl.program_id(0)
    n = pl.num_programs(0)
    tt = x_ref.shape[0]

    def issue(step):
        sl = step % 2

        def body(u, carry):
            for q in range(DMA_UNROLL // TOP_K):
                t = u * (DMA_UNROLL // TOP_K) + q
                for k in range(TOP_K):
                    p = pos_ref[(step * tt + t) * TOP_K + k]
                    pltpu.make_async_copy(y_hbm.at[p], ybuf_ref.at[sl, k, t], sem.at[sl]).start()
            return carry

        lax.fori_loop(0, tt * TOP_K // DMA_UNROLL, body, 0)

    @pl.when(i == 0)
    def _():
        issue(i)

    @pl.when(i + 1 < n)
    def _():
        issue(i + 1)

    slot = i % 2
    pltpu.make_async_copy(ybuf_ref.at[slot], ybuf_ref.at[slot], sem.at[slot]).wait()
    gates = g_ref[...]
    tot = None
    for k in range(TOP_K):
        term = gates[:, k:k + 1] * ybuf_ref[slot, k]
        tot = term if tot is None else tot + term
    o_ref[...] = x_ref[...] + gate_ref[0] * tot


def _moe_combine(x, y, pos, gates, gate_mod):
    b, s, d = x.shape
    tt = COMBINE_TOKENS
    n_tok = b * s
    grid_spec = pltpu.PrefetchScalarGridSpec(
        num_scalar_prefetch=1,
        grid=(n_tok // tt,),
        in_specs=[
            pl.BlockSpec((tt, d), lambda i, pos_r: (i, 0)),
            pl.BlockSpec((tt, TOP_K), lambda i, pos_r: (i, 0)),
            pl.BlockSpec((1, 1, d), lambda i, pos_r: (i * tt // s, 0, 0)),
            pl.BlockSpec(memory_space=pl.ANY),
        ],
        out_specs=pl.BlockSpec((tt, d), lambda i, pos_r: (i, 0)),
        scratch_shapes=[pltpu.VMEM((2, TOP_K, tt, d), F32), pltpu.SemaphoreType.DMA((2,))],
    )
    out = pl.pallas_call(
        _moe_combine_kernel,
        grid_spec=grid_spec,
        out_shape=jax.ShapeDtypeStruct((n_tok, d), F32),
        compiler_params=_cparams(("arbitrary",)),
        name="moe_combine",
    )(pos, x.reshape(n_tok, d), gates.reshape(n_tok, TOP_K), gate_mod, y)
    return out.reshape(b, s, d)


def _moe_ffn(x, g, scale, shift, gate_mod, rw, rb, w_gu, b_gu, w_dn, b_dn, layer, y_buf):
    b, s, d = x.shape
    h, top_idx, gates = _norm_router(x, g, scale, shift, rw, rb)
    item_e, item_row0, item_nsub, slot_tok, pos = _moe_plan(top_idx.reshape(b * s, TOP_K), b * s)
    if y_buf is None:
        y_buf = jnp.zeros((slot_tok.shape[0], d), F32)
    y = _moe_experts(h.reshape(b * s, d), y_buf, (item_e, item_row0, item_nsub, slot_tok),
                     w_gu, b_gu, w_dn, b_dn, layer)
    return _moe_combine(x, y, pos, gates, gate_mod), y


def _rope_lane_tables(seq):
    inv_freq = ROPE_THETA ** (-jnp.arange(0, HEAD_DIM, 2, dtype=F32) / HEAD_DIM)
    ang = jnp.arange(seq, dtype=F32)[:, None] * inv_freq[None, :]
    cos, sin = jnp.cos(ang), jnp.sin(ang)
    reps = LANES // HEAD_DIM
    return (jnp.tile(jnp.concatenate([cos, cos], axis=-1), (1, reps)),
            jnp.tile(jnp.concatenate([-sin, sin], axis=-1), (1, reps)))


def _col_modes(kinds, tn, width):
    modes = []
    for n_cols, mode in kinds:
        assert n_cols % tn == 0
        modes += [mode] * (n_cols // tn)
    assert len(modes) * tn == width
    return jnp.asarray(modes, jnp.int32)


PLAIN, ROPE, SCALE, ROPE_SCALE = 0, 1, 2, 3


def kernel(x, c, ada_w, ada_b, norm_g, final_g, ab_w_in, ab_w_out, na_rpb, diff_lam_q1, diff_lam_k1,
           diff_lam_q2, diff_lam_k2, diff_subln_g, dil_w_in, dil_w_out, moe_router_w, moe_router_b,
           moe_w_gate_up, moe_b_gate_up, moe_w_down, moe_b_down):
    b, s, d = x.shape
    depth = ada_w.shape[0]
    tn = PROJ_TN
    cos_t, sin_t = _rope_lane_tables(s)
    y_buf = None

    c_pad = jnp.zeros((8, d), F32).at[:b].set(c)
    mod = _adaln(c_pad, ada_w.reshape(depth * 2, d, 3 * d), ada_b.reshape(depth * 2, 3 * d))
    mod = mod[:, :b].reshape(depth, 2, b, 3, 1, d)

    na_w = NA_HEADS * HEAD_DIM
    df_w = DIFF_HEADS * 2 * HEAD_DIM
    dil_w = DIL_HEADS * HEAD_DIM

    for l in range(depth):
        shift, scale, gate = (mod[l, 0, :, t] for t in range(3))
        h = _norm_mod(x, norm_g[l, 0], scale, shift, BF16)
        i = l // 2
        if l % 2 == 0:
            lambda_init = 0.8 - 0.6 * math.exp(-0.3 * l)
            modes = _col_modes([(na_w, SCALE), (2 * na_w, PLAIN), (df_w, ROPE_SCALE), (df_w, ROPE),
                                (df_w, PLAIN)], tn, ab_w_in.shape[2])
            proj = _proj(h, ab_w_in[i], modes, cos_t, sin_t, ab_w_in.shape[2], tn=tn)
            proj = proj.reshape(b, s, ab_w_in.shape[2])
            o_na = _na_attention(proj, _na_bias_table(na_rpb[i], s // GRID_W))
            o_df = _diff_attention(proj, diff_lam_q1[i], diff_lam_k1[i], diff_lam_q2[i], diff_lam_k2[i],
                                   diff_subln_g[i], lambda_init, col0=3 * na_w // LANES)
            o = jnp.concatenate([o_na, o_df], axis=-1)
            x = _outproj(o, ab_w_out[i], x, gate)
        else:
            modes = _col_modes([(dil_w, ROPE_SCALE), (dil_w, ROPE), (dil_w, PLAIN)] * len(DIL_CONFIGS),
                               tn, dil_w_in.shape[2])
            outs, lses = [], []
            for gi, (window, dil) in enumerate(DIL_CONFIGS):
                proj = _proj(h, dil_w_in[i], modes, cos_t, sin_t, 3 * dil_w, jbase=gi * (3 * dil_w // tn),
                             dil=dil, tn=tn)
                o_g, lse_g = _dil_attention(proj, window, dil)
                outs.append(o_g)
                lses.append(lse_g)
            o = _dil_mix(outs, lses, [dil for _, dil in DIL_CONFIGS])
            x = _outproj(o, dil_w_out[i], x, gate)
        shift, scale, gate = (mod[l, 1, :, t] for t in range(3))
        x, y_buf = _moe_ffn(x, norm_g[l, 1], scale, shift, gate, moe_router_w[l], moe_router_b[l],
                            moe_w_gate_up, moe_b_gate_up, moe_w_down, moe_b_down, l, y_buf)
    return _final_norm(x, final_g)
```

```python
import functools
import math

import jax
import jax.numpy as jnp
from jax import lax
from jax.experimental import pallas as pl
from jax.experimental.pallas import tpu as pltpu

F32 = jnp.float32
BF16 = jnp.bfloat16

HEAD_DIM = 64
LANES = 128
GRID_W = 64
NA_HEADS = 16
NA_ROWS = 8
NA_COLS = 16
DIFF_HEADS = 8
DIL_HEADS = 32
DIL_CONFIGS = ((128, 1), (512, 4), (2048, 16))
ROPE_THETA = 10000.0
N_EXPERTS = 32
TOP_K = 4
SWIGLU_LIMIT = 7.0
SWIGLU_ALPHA = 1.702
RMS_EPS = 1e-5

PROJ_MCHUNKS = 4
DIL_QBLOCK = 128
DIL_BLOCK_UNROLL = 4
NA_ROW_UNROLL = 4
MOE_SUB = 256
MOE_NSUB = 5
MOE_ITEM_ROWS = MOE_NSUB * MOE_SUB
MOE_FT = 128
MOE_TILE = 2 * MOE_FT
PROJ_TN = 1024
MOE_GU_KCHUNKS = 4
GATHER_SHARE = MOE_SUB // 2
DMA_UNROLL = 8
COMBINE_TOKENS = 128
VMEM_LIMIT = 56 * 1024 * 1024


def _cparams(sem, vmem=VMEM_LIMIT):
    return pltpu.CompilerParams(dimension_semantics=sem, vmem_limit_bytes=vmem)


def _dot(a, b):
    return jnp.dot(a, b, preferred_element_type=F32)


def _dot_nt(a, b):
    return lax.dot_general(a, b, (((1,), (1,)), ((), ())), preferred_element_type=F32)


def _split_bf16(a):
    hi = a.astype(BF16)
    lo = (a - hi.astype(F32)).astype(BF16)
    return hi, lo


def _dot3(a, b):
    ah, al = _split_bf16(a)
    bh, bl = _split_bf16(b)
    return _dot(ah, bh) + (_dot(ah, bl) + _dot(al, bh))


def _adaln_kernel(c_ref, w_ref, b_ref, o_ref):
    c = c_ref[...]
    c_act = c * jax.nn.sigmoid(c)
    o_ref[0] = _dot3(c_act, w_ref[0]) + b_ref[0]


def _adaln(c_pad, w, b, tn=768):
    n_mod, d, n = w.shape
    return pl.pallas_call(
        _adaln_kernel,
        grid=(n_mod, n // tn),
        in_specs=[
            pl.BlockSpec(c_pad.shape, lambda l, j: (0, 0)),
            pl.BlockSpec((1, d, tn), lambda l, j: (l, 0, j)),
            pl.BlockSpec((1, 1, tn), lambda l, j: (l, 0, j)),
        ],
        out_specs=pl.BlockSpec((1, c_pad.shape[0], tn), lambda l, j: (l, 0, j)),
        out_shape=jax.ShapeDtypeStruct((n_mod, c_pad.shape[0], n), F32),
        compiler_params=_cparams(("arbitrary", "arbitrary")),
        name="adaln",
    )(c_pad, w, b.reshape(n_mod, 1, n))


def _rms(x, g):
    return x * lax.rsqrt(jnp.mean(x * x, axis=-1, keepdims=True) + RMS_EPS) * g


def _norm_mod_kernel(x_ref, g_ref, scale_ref, shift_ref, o_ref):
    h = _rms(x_ref[0], g_ref[...]) * (1.0 + scale_ref[0]) + shift_ref[0]
    o_ref[0] = h.astype(o_ref.dtype)


def _norm_mod(x, g, scale, shift, out_dtype, tm=512):
    b, s, d = x.shape
    return pl.pallas_call(
        _norm_mod_kernel,
        grid=(b, s // tm),
        in_specs=[
            pl.BlockSpec((1, tm, d), lambda bi, i: (bi, i, 0)),
            pl.BlockSpec((1, d), lambda bi, i: (0, 0)),
            pl.BlockSpec((1, 1, d), lambda bi, i: (bi, 0, 0)),
            pl.BlockSpec((1, 1, d), lambda bi, i: (bi, 0, 0)),
        ],
        out_specs=pl.BlockSpec((1, tm, d), lambda bi, i: (bi, i, 0)),
        out_shape=jax.ShapeDtypeStruct((b, s, d), out_dtype),
        compiler_params=_cparams(("arbitrary", "arbitrary")),
        name="norm_mod",
    )(x, g.reshape(1, d), scale, shift)


def _final_norm_kernel(x_ref, g_ref, o_ref):
    o_ref[0] = _rms(x_ref[0], g_ref[...])


def _final_norm(x, g, tm=512):
    b, s, d = x.shape
    return pl.pallas_call(
        _final_norm_kernel,
        grid=(b, s // tm),
        in_specs=[
            pl.BlockSpec((1, tm, d), lambda bi, i: (bi, i, 0)),
            pl.BlockSpec((1, d), lambda bi, i: (0, 0)),
        ],
        out_specs=pl.BlockSpec((1, tm, d), lambda bi, i: (bi, i, 0)),
        out_shape=jax.ShapeDtypeStruct((b, s, d), F32),
        compiler_params=_cparams(("arbitrary", "arbitrary")),
        name="final_norm",
    )(x, g.reshape(1, d))


def _proj_kernel(mode_ref, a_ref, w_ref, cos_ref, sin_ref, o_ref, wbf_ref, stage_ref, *, jbase, dil):
    j = pl.program_id(0)

    @pl.when((pl.program_id(1) == 0) & (pl.program_id(2) == 0))
    def _():
        wbf_ref[...] = w_ref[...].astype(BF16)

    mode = mode_ref[jbase + j]
    tm = a_ref.shape[1]
    tn = wbf_ref.shape[1]
    cm = tm // PROJ_MCHUNKS

    def emit(mi, ci, val):
        cols = slice(ci * LANES, (ci + 1) * LANES)
        if dil == 1:
            o_ref[0, 0, mi * cm:(mi + 1) * cm, cols] = val.astype(o_ref.dtype)
        else:
            stage_ref[ci, mi * cm:(mi + 1) * cm, :] = val
            n = cm // dil
            for r in range(dil):
                o_ref[0, r, mi * n:(mi + 1) * n, cols] = (
                    stage_ref[ci, pl.ds(mi * cm + r, n, stride=dil), :].astype(o_ref.dtype))

    @pl.when(mode == 0)
    def _():
        for mi in range(PROJ_MCHUNKS):
            acc = _dot(a_ref[0, mi * cm:(mi + 1) * cm, :], wbf_ref[...])
            for ci in range(tn // LANES):
                emit(mi, ci, acc[:, ci * LANES:(ci + 1) * LANES])

    @pl.when(mode != 0)
    def _():
        rope_on = (mode & 1).astype(F32)
        qscale = jnp.where((mode & 2) == 2, HEAD_DIM ** -0.5, 1.0).astype(F32)
        lane = lax.broadcasted_iota(jnp.int32, (1, LANES), 1)
        first_half = (lane % HEAD_DIM) < (HEAD_DIM // 2)
        for mi in range(PROJ_MCHUNKS):
            rows = slice(mi * cm, (mi + 1) * cm)
            acc = _dot(a_ref[0, rows, :], wbf_ref[...])
            cos = (cos_ref[rows, :] * rope_on + (1.0 - rope_on)) * qscale
            sin = sin_ref[rows, :] * (rope_on * qscale)
            for ci in range(tn // LANES):
                x = acc[:, ci * LANES:(ci + 1) * LANES]
                partner = jnp.where(first_half,
                                    pltpu.roll(x, LANES - HEAD_DIM // 2, 1),
                                    pltpu.roll(x, HEAD_DIM // 2, 1))
                emit(mi, ci, x * cos + partner * sin)


def _proj(a, w, modes, cos_t, sin_t, n_out, jbase=0, dil=1, tm=1024, tn=PROJ_TN):
    b, s, k = a.shape
    grid_spec = pltpu.PrefetchScalarGridSpec(
        num_scalar_prefetch=1,
        grid=(n_out // tn, b, s // tm),
        in_specs=[
            pl.BlockSpec((1, tm, k), lambda j, bi, i, m: (bi, i, 0)),
            pl.BlockSpec((k, tn), lambda j, bi, i, m: (0, jbase + j)),
            pl.BlockSpec((tm, LANES), lambda j, bi, i, m: (i, 0)),
            pl.BlockSpec((tm, LANES), lambda j, bi, i, m: (i, 0)),
        ],
        out_specs=pl.BlockSpec((1, dil, tm // dil, tn), lambda j, bi, i, m: (bi, 0, i, j)),
        scratch_shapes=[pltpu.VMEM((k, tn), BF16), pltpu.VMEM((tn // LANES, tm, LANES), F32)],
    )
    return pl.pallas_call(
        functools.partial(_proj_kernel, jbase=jbase, dil=dil),
        grid_spec=grid_spec,
        out_shape=jax.ShapeDtypeStruct((b, dil, s // dil, n_out), BF16),
        compiler_params=_cparams(("arbitrary", "arbitrary", "arbitrary")),
        name="proj",
    )(modes, a, w, cos_t, sin_t)


def _outproj_kernel(a_ref, w_ref, x_ref, gate_ref, o_ref, wbf_ref):
    @pl.when((pl.program_id(1) == 0) & (pl.program_id(2) == 0))
    def _():
        wbf_ref[...] = w_ref[...].astype(BF16)

    o_ref[0] = x_ref[0] + gate_ref[0] * _dot(a_ref[0], wbf_ref[...])


def _outproj(a, w, x, gate, tm=1024, tn=512):
    b, s, k = a.shape
    n = w.shape[1]
    return pl.pallas_call(
        _outproj_kernel,
        grid=(n // tn, b, s // tm),
        in_specs=[
            pl.BlockSpec((1, tm, k), lambda j, bi, i: (bi, i, 0)),
            pl.BlockSpec((k, tn), lambda j, bi, i: (0, j)),
            pl.BlockSpec((1, tm, tn), lambda j, bi, i: (bi, i, j)),
            pl.BlockSpec((1, 1, tn), lambda j, bi, i: (bi, 0, j)),
        ],
        out_specs=pl.BlockSpec((1, tm, tn), lambda j, bi, i: (bi, i, j)),
        out_shape=jax.ShapeDtypeStruct((b, s, n), F32),
        scratch_shapes=[pltpu.VMEM((k, tn), BF16)],
        compiler_params=_cparams(("arbitrary", "arbitrary", "arbitrary")),
        name="outproj",
    )(a, w, x, gate)


def _columns(cols):
    m = cols[0].shape[0]
    lane = lax.broadcasted_iota(jnp.int32, (m, len(cols)), 1)
    out = jnp.broadcast_to(cols[0], (m, len(cols)))
    for k in range(1, len(cols)):
        out = jnp.where(lane == k, cols[k], out)
    return out


def _head_masks():
    lane = lax.broadcasted_iota(jnp.int32, (1, LANES), 1)
    return lane < HEAD_DIM, lane >= HEAD_DIM


def _na_kernel(q_ref, k_ref, v_ref, bias_ref, o_ref, *, rows):
    kr = min(NA_ROWS, rows)
    masks = _head_masks()

    def body(t, carry):
        r_list = [t * NA_ROW_UNROLL + u for u in range(NA_ROW_UNROLL)]
        scores, values = [], []
        for r in r_list:
            rs = jnp.clip(r - kr // 2, 0, rows - kr)
            var = r - rs
            q = q_ref[0, pl.ds(pl.multiple_of(r * GRID_W, GRID_W), GRID_W), :]
            k = k_ref[0, pl.ds(pl.multiple_of(rs * GRID_W, GRID_W), kr * GRID_W), :]
            values.append(v_ref[0, pl.ds(pl.multiple_of(rs * GRID_W, GRID_W), kr * GRID_W), :])
            for h in range(2):
                qh = jnp.where(masks[h], q, jnp.zeros_like(q))
                scores.append(_dot_nt(qh, k) + bias_ref[h, var])
        probs, dens = [], []
        for sc in scores:
            e = jnp.exp(sc - jnp.max(sc, axis=-1, keepdims=True))
            dens.append(jnp.sum(e, axis=-1, keepdims=True))
            probs.append(e.astype(BF16))
        for u, r in enumerate(r_list):
            outs = [_dot(probs[2 * u + h], values[u]) / dens[2 * u + h] for h in range(2)]
            o = jnp.where(masks[0], outs[0], outs[1])
            o_ref[0, pl.ds(pl.multiple_of(r * GRID_W, GRID_W), GRID_W), :] = o.astype(o_ref.dtype)
        return carry

    assert rows % NA_ROW_UNROLL == 0
    lax.fori_loop(0, rows // NA_ROW_UNROLL, body, 0)


def _na_bias_table(rpb, rows):
    kr = min(NA_ROWS, rows)
    c = jnp.arange(GRID_W)
    kc = jnp.arange(GRID_W)
    cs = jnp.clip(c - NA_COLS // 2, 0, GRID_W - NA_COLS)
    valid = (kc[None, :] >= cs[:, None]) & (kc[None, :] < cs[:, None] + NA_COLS)
    coff = jnp.clip(kc[None, :] - c[:, None] + (NA_COLS - 1), 0, 2 * NA_COLS - 2)
    rows_v = jnp.stack([rpb[:, NA_ROWS - 1 - v:NA_ROWS - 1 - v + kr, :] for v in range(kr)], axis=1)
    onehot = ((coff[None] == jnp.arange(2 * NA_COLS - 1)[:, None, None]) & valid[None]).astype(F32)
    tbl = jnp.einsum('hvio,ock->hvcik', rows_v.astype(F32), onehot, precision=lax.Precision.HIGHEST)
    tbl = jnp.where(valid[None, None, :, None, :], tbl, -jnp.inf)
    return tbl.reshape(rpb.shape[0], kr, GRID_W, kr * GRID_W)


def _na_attention(proj, bias_tbl):
    b, s, _ = proj.shape
    rows = s // GRID_W
    npair = NA_HEADS // 2
    kr = bias_tbl.shape[1]
    blk = lambda off: pl.BlockSpec((1, s, LANES), lambda bi, c: (bi, 0, off + c))
    return pl.pallas_call(
        functools.partial(_na_kernel, rows=rows),
        grid=(b, npair),
        in_specs=[
            blk(0), blk(npair), blk(2 * npair),
            pl.BlockSpec((2, kr, GRID_W, kr * GRID_W), lambda bi, c: (c, 0, 0, 0)),
        ],
        out_specs=pl.BlockSpec((1, s, LANES), lambda bi, c: (bi, 0, c)),
        out_shape=jax.ShapeDtypeStruct((b, s, NA_HEADS * HEAD_DIM), BF16),
        compiler_params=_cparams(("arbitrary", "arbitrary")),
        name="na_attn",
    )(proj, proj, proj, bias_tbl)


def _diff_kernel(lam_ref, q_ref, k_ref, v_ref, g_ref, o_ref, *, tk, lambda_init):
    masks = _head_masks()
    q = q_ref[0]
    tq = q.shape[0]
    s = k_ref.shape[1]
    qs = [jnp.where(masks[m], q, jnp.zeros_like(q)) for m in range(2)]

    def body(j, carry):
        k = k_ref[0, pl.ds(pl.multiple_of(j * tk, tk), tk), :]
        v = v_ref[0, pl.ds(pl.multiple_of(j * tk, tk), tk), :]
        scores = [_dot_nt(qs[m], k) for m in range(2)]
        stats = []
        for m in range(2):
            m_old, l_old = carry[3 * m], carry[3 * m + 1]
            m_new = jnp.maximum(m_old, jnp.max(scores[m], axis=-1, keepdims=True))
            alpha = jnp.exp(m_old - m_new)
            e = jnp.exp(scores[m] - m_new)
            l_new = alpha * l_old + jnp.sum(e, axis=-1, keepdims=True)
            stats.append((m_new, l_new, alpha, e.astype(BF16)))
        new = []
        for m in range(2):
            m_new, l_new, alpha, p = stats[m]
            new += [m_new, l_new, alpha * carry[3 * m + 2] + _dot(p, v)]
        return tuple(new)

    init = []
    for m in range(2):
        init += [jnp.full((tq, 1), -jnp.inf, F32), jnp.zeros((tq, 1), F32), jnp.zeros((tq, LANES), F32)]
    res = lax.fori_loop(0, s // tk, body, tuple(init))
    lam = lam_ref[0]
    o = res[2] / res[1] - lam * (res[5] / res[4])
    o_ref[0] = (_rms(o, g_ref[...]) * (1.0 - lambda_init)).astype(o_ref.dtype)


def _diff_lambda_kernel(lq1_ref, lk1_ref, lq2_ref, lk2_ref, o_ref, *, lambda_init):
    s1 = jnp.sum(lq1_ref[...] * lk1_ref[...], axis=-1, keepdims=True)
    s2 = jnp.sum(lq2_ref[...] * lk2_ref[...], axis=-1, keepdims=True)
    o_ref[...] = jnp.exp(s1) - jnp.exp(s2) + lambda_init


def _diff_attention(proj, lq1, lk1, lq2, lk2, subln_g, lambda_init, col0, tq=512, tk=512):
    b, s, _ = proj.shape
    nh = DIFF_HEADS
    vec = lambda a: a.reshape(1, HEAD_DIM).astype(F32)
    lam = pl.pallas_call(
        functools.partial(_diff_lambda_kernel, lambda_init=lambda_init),
        out_shape=jax.ShapeDtypeStruct((1, 1), F32),
        name="diff_lambda",
    )(vec(lq1), vec(lk1), vec(lq2), vec(lk2)).reshape(1)
    grid_spec = pltpu.PrefetchScalarGridSpec(
        num_scalar_prefetch=1,
        grid=(b, nh, s // tq),
        in_specs=[
            pl.BlockSpec((1, tq, LANES), lambda bi, h, i, lam_r: (bi, i, col0 + h)),
            pl.BlockSpec((1, s, LANES), lambda bi, h, i, lam_r: (bi, 0, col0 + nh + h)),
            pl.BlockSpec((1, s, LANES), lambda bi, h, i, lam_r: (bi, 0, col0 + 2 * nh + h)),
            pl.BlockSpec((1, LANES), lambda bi, h, i, lam_r: (0, 0)),
        ],
        out_specs=pl.BlockSpec((1, tq, LANES), lambda bi, h, i, lam_r: (bi, i, h)),
    )
    return pl.pallas_call(
        functools.partial(_diff_kernel, tk=tk, lambda_init=lambda_init),
        grid_spec=grid_spec,
        out_shape=jax.ShapeDtypeStruct((b, s, nh * LANES), BF16),
        compiler_params=_cparams(("arbitrary", "arbitrary", "arbitrary")),
        name="diff_attn",
    )(lam, proj, proj, proj, subln_g.reshape(1, LANES).astype(F32))


def _dil_kernel(q_ref, k_ref, v_ref, o_ref, lse_ref, *, n_side):
    masks = _head_masks()
    length = q_ref.shape[2]
    tq = DIL_QBLOCK
    tk = tq + 2 * n_side
    rel0 = (lax.broadcasted_iota(jnp.int32, (tq, tk), 1)
            - lax.broadcasted_iota(jnp.int32, (tq, tk), 0))

    nq = length // tq
    planes = q_ref.shape[1]

    def body(t, carry):
        where, scores, values = [], [], []
        for u in range(DIL_BLOCK_UNROLL):
            bi = t * DIL_BLOCK_UNROLL + u
            p = bi // nq
            a0 = pl.multiple_of((bi % nq) * tq, tq)
            ks = pl.multiple_of(jnp.clip(a0 - n_side, 0, length - tk), n_side)
            q = q_ref[0, p, pl.ds(a0, tq), :]
            k = k_ref[0, p, pl.ds(ks, tk), :]
            values.append(v_ref[0, p, pl.ds(ks, tk), :])
            where.append((p, a0))
            rel = rel0 + (ks - a0)
            band = (rel >= -n_side) & (rel <= n_side)
            for h in range(2):
                qh = jnp.where(masks[h], q, jnp.zeros_like(q))
                scores.append(jnp.where(band, _dot_nt(qh, k), -jnp.inf))
        probs, dens, lses = [], [], []
        for sc in scores:
            m = jnp.max(sc, axis=-1, keepdims=True)
            e = jnp.exp(sc - m)
            den = jnp.sum(e, axis=-1, keepdims=True)
            probs.append(e.astype(BF16))
            dens.append(den)
            lses.append(m + jnp.log(den))
        for u, (p, a0) in enumerate(where):
            outs = [_dot(probs[2 * u + h], values[u]) / dens[2 * u + h] for h in range(2)]
            o_ref[0, p, pl.ds(a0, tq), :] = jnp.where(masks[0], outs[0], outs[1]).astype(o_ref.dtype)
            lse_ref[0, 0, p, pl.ds(a0, tq), :] = _columns(lses[2 * u:2 * u + 2])
        return carry

    assert (planes * nq) % DIL_BLOCK_UNROLL == 0
    lax.fori_loop(0, planes * nq // DIL_BLOCK_UNROLL, body, 0)


def _dil_attention(proj, window, dil):
    b, _, length, _ = proj.shape
    n_side = (window // 2) // dil
    npair = DIL_HEADS // 2
    cw = DIL_HEADS * HEAD_DIM
    rp = min(dil, max(1, DIL_BLOCK_UNROLL * DIL_QBLOCK // length))
    blk = lambda off: pl.BlockSpec((1, rp, length, LANES), lambda bi, c, r: (bi, r, 0, off + c))
    return pl.pallas_call(
        functools.partial(_dil_kernel, n_side=n_side),
        grid=(b, npair, dil // rp),
        in_specs=[blk(0), blk(npair), blk(2 * npair)],
        out_specs=[
            pl.BlockSpec((1, rp, length, LANES), lambda bi, c, r: (bi, r, 0, c)),
            pl.BlockSpec((1, 1, rp, length, 2), lambda bi, c, r: (c, bi, r, 0, 0)),
        ],
        out_shape=[
            jax.ShapeDtypeStruct((b, dil, length, cw), BF16),
            jax.ShapeDtypeStruct((npair, b, dil, length, 2), F32),
        ],
        compiler_params=_cparams(("arbitrary", "arbitrary", "arbitrary")),
        name="dil_attn",
    )(proj, proj, proj)


def _dil_mix_kernel(*refs, dils):
    ng = len(dils)
    o_refs, l_refs = refs[:ng], refs[ng:2 * ng]
    out_ref, so_ref, sl_ref = refs[2 * ng:]
    masks = _head_masks()
    tt = out_ref.shape[1]

    def widen(lse):
        return jnp.where(masks[0], lse[:, 0:1], lse[:, 1:2])

    outs, lses = [], []
    for g, dil in enumerate(dils):
        if dil == 1:
            outs.append(o_refs[g][0, 0].astype(F32))
            lses.append(widen(l_refs[g][0, 0, 0]))
        else:
            n = tt // dil
            for r in range(dil):
                so_ref[g, pl.ds(r, n, stride=dil), :] = o_refs[g][0, r].astype(F32)
                sl_ref[g, pl.ds(r, n, stride=dil), :] = widen(l_refs[g][0, 0, r])
            outs.append(so_ref[g])
            lses.append(sl_ref[g])
    mx = lses[0]
    for l in lses[1:]:
        mx = jnp.maximum(mx, l)
    ws = [jnp.exp(l - mx) for l in lses]
    tot = ws[0]
    for w in ws[1:]:
        tot = tot + w
    acc = None
    for w, o in zip(ws, outs):
        term = (w / tot) * o
        acc = term if acc is None else acc + term
    out_ref[0] = acc.astype(out_ref.dtype)


def _dil_mix(outs, lses, dils, tt=1024):
    b, _, _, cw = outs[0].shape
    s = outs[0].shape[1] * outs[0].shape[2]
    npair = cw // LANES
    ng = len(dils)
    o_specs = [pl.BlockSpec((1, d, tt // d, LANES), lambda bi, i, c: (bi, 0, i, c)) for d in dils]
    l_specs = [pl.BlockSpec((1, 1, d, tt // d, 2), lambda bi, i, c: (c, bi, 0, i, 0)) for d in dils]
    return pl.pallas_call(
        functools.partial(_dil_mix_kernel, dils=tuple(dils)),
        grid=(b, s // tt, npair),
        in_specs=o_specs + l_specs,
        out_specs=pl.BlockSpec((1, tt, LANES), lambda bi, i, c: (bi, i, c)),
        out_shape=jax.ShapeDtypeStruct((b, s, cw), BF16),
        scratch_shapes=[pltpu.VMEM((ng, tt, LANES), F32), pltpu.VMEM((ng, tt, LANES), F32)],
        compiler_params=_cparams(("arbitrary", "arbitrary", "arbitrary")),
        name="dil_mix",
    )(*outs, *lses)


def _norm_router_kernel(x_ref, g_ref, scale_ref, shift_ref, rw_ref, rb_ref, h_ref, idx_ref, gate_ref):
    h = _rms(x_ref[0], g_ref[...]) * (1.0 + scale_ref[0]) + shift_ref[0]
    h_ref[0] = h
    logits = _dot3(h, rw_ref[...]) + rb_ref[...]
    n_e = logits.shape[1]
    eidx = lax.broadcasted_iota(jnp.int32, logits.shape, 1).astype(F32)
    vals, idxs = [], []
    for _ in range(TOP_K):
        m = jnp.max(logits, axis=-1, keepdims=True)
        first = jnp.min(jnp.where(logits == m, eidx, float(n_e)), axis=-1, keepdims=True)
        vals.append(m)
        idxs.append(first)
        logits = jnp.where(eidx == first, -jnp.inf, logits)
    top = _columns(vals)
    e = jnp.exp(top - vals[0])
    gate_ref[0] = e / jnp.sum(e, axis=-1, keepdims=True)
    idx_ref[0] = _columns(idxs).astype(jnp.int32)


def _norm_router(x, g, scale, shift, rw, rb, tm=512):
    b, s, d = x.shape
    n_e = rw.shape[1]
    tok = lambda w: pl.BlockSpec((1, tm, w), lambda bi, i: (bi, i, 0))
    return pl.pallas_call(
        _norm_router_kernel,
        grid=(b, s // tm),
        in_specs=[
            tok(d),
            pl.BlockSpec((1, d), lambda bi, i: (0, 0)),
            pl.BlockSpec((1, 1, d), lambda bi, i: (bi, 0, 0)),
            pl.BlockSpec((1, 1, d), lambda bi, i: (bi, 0, 0)),
            pl.BlockSpec((d, n_e), lambda bi, i: (0, 0)),
            pl.BlockSpec((1, n_e), lambda bi, i: (0, 0)),
        ],
        out_specs=[tok(d), tok(TOP_K), tok(TOP_K)],
        out_shape=[
            jax.ShapeDtypeStruct((b, s, d), F32),
            jax.ShapeDtypeStruct((b, s, TOP_K), jnp.int32),
            jax.ShapeDtypeStruct((b, s, TOP_K), F32),
        ],
        compiler_params=_cparams(("arbitrary", "arbitrary")),
        name="norm_router",
    )(x, g.reshape(1, d), scale, shift, rw, rb.reshape(1, n_e))


def _moe_plan(top_idx, n_tok):
    n_assign = n_tok * TOP_K
    flat_e = top_idx.reshape(-1)
    onehot = (flat_e[:, None] == jnp.arange(N_EXPERTS, dtype=jnp.int32)[None, :]).astype(jnp.int32)
    csum = jnp.cumsum(onehot, axis=0)
    counts = csum[-1]
    rank = jnp.take_along_axis(csum, flat_e[:, None], axis=1)[:, 0] - 1
    padded = (counts + MOE_SUB - 1) // MOE_SUB * MOE_SUB
    pstarts = jnp.cumsum(padded) - padded
    dest = pstarts[flat_e] + rank
    n_slots = n_assign + N_EXPERTS * MOE_SUB
    slot_tok = jnp.zeros((n_slots,), jnp.int32).at[dest].set(
        jnp.arange(n_assign, dtype=jnp.int32) // TOP_K)
    n_items = (n_slots + N_EXPERTS * (MOE_ITEM_ROWS - MOE_SUB)) // MOE_ITEM_ROWS
    items_per_e = (padded + MOE_ITEM_ROWS - 1) // MOE_ITEM_ROWS
    item_ends = jnp.cumsum(items_per_e)
    total_items = item_ends[-1]
    i = jnp.arange(n_items, dtype=jnp.int32)
    i_eff = jnp.minimum(i, total_items - 1)
    e_i = jnp.clip(jnp.searchsorted(item_ends, i_eff, side='right'), 0, N_EXPERTS - 1).astype(jnp.int32)
    local = i_eff - (item_ends[e_i] - items_per_e[e_i])
    row0 = pstarts[e_i] + local * MOE_ITEM_ROWS
    nrows = jnp.clip(padded[e_i] - local * MOE_ITEM_ROWS, 0, MOE_ITEM_ROWS)
    nsub = jnp.where(i < total_items, nrows // MOE_SUB, 0)
    return (e_i.astype(jnp.int32), row0.astype(jnp.int32), nsub.astype(jnp.int32),
            slot_tok, dest.astype(jnp.int32))


def _swiglu_pair(h_a, h_b):
    n = h_a.shape[1]
    lane = lax.broadcasted_iota(jnp.int32, (1, n), 1)
    even = (lane % 2) == 0

    def glu_lin(h):
        glu = jnp.minimum(h, SWIGLU_LIMIT)
        glu = glu * jax.nn.sigmoid(SWIGLU_ALPHA * glu)
        lin = jnp.clip(h, -SWIGLU_LIMIT, SWIGLU_LIMIT) + 1.0
        return glu, lin

    glu_a, lin_a = glu_lin(h_a)
    glu_b, lin_b = glu_lin(h_b)
    act_a = glu_a * pltpu.roll(lin_a, n - 1, 1)
    act_b = pltpu.roll(glu_b, 1, 1) * lin_b
    return jnp.where(even, act_a, act_b)


def _for_each_sub(nsub, prep, fn):
    for count in range(1, MOE_NSUB + 1):
        @pl.when(nsub == count)
        def _():
            prep()
            for sb in range(count):
                fn(sb)


def _moe_kernel(ie_ref, row0_ref, nsub_ref, stok_ref,
                h_hbm, y_init_hbm, wgu_a_ref, wgu_b_ref, bgu_a_ref, bgu_b_ref, wdn_ref, bdn_ref,
                y_hbm,
                xs_ref, act_ref, ot_ref, *scratch, n_gu):
    del y_init_hbm
    wgu_bf_refs = scratch[:MOE_GU_KCHUNKS]
    il_refs = scratch[MOE_GU_KCHUNKS:MOE_GU_KCHUNKS + n_gu]
    wdn_bf_refs = scratch[MOE_GU_KCHUNKS + n_gu:MOE_GU_KCHUNKS + 2 * n_gu]
    xsem, osem = scratch[MOE_GU_KCHUNKS + 2 * n_gu:]
    i = pl.program_id(0)
    s = pl.program_id(1)
    n_items = pl.num_programs(0)
    n_steps = pl.num_programs(1)
    nsub = nsub_ref[i]
    row0 = row0_ref[i]
    slot = i % 2
    ft2 = 2 * MOE_FT

    def sub_rows(sb):
        return slice(sb * MOE_SUB, (sb + 1) * MOE_SUB)

    def gather_rows(item, sl, first, count):
        def body(u, carry):
            for q in range(DMA_UNROLL):
                r = first + u * DMA_UNROLL + q
                tok = stok_ref[row0_ref[item] + r]
                pltpu.make_async_copy(h_hbm.at[tok], xs_ref.at[sl, r], xsem.at[sl]).start()
            return carry

        lax.fori_loop(0, count // DMA_UNROLL, body, 0)

    def gather_wait(sl, sb):
        dst = xs_ref.at[sl, pl.ds(sb * MOE_SUB, MOE_SUB)]
        pltpu.make_async_copy(dst, dst, xsem.at[sl]).wait()

    def for_item_subs(item, fn):
        for sb in range(MOE_NSUB):
            @pl.when(sb < nsub_ref[item])
            def _():
                fn(sb)

    @pl.when(s == 0)
    def _own_rows():
        @pl.when(i == 0)
        def _():
            gather_rows(i, slot, 0, nsub * MOE_SUB)

        for_item_subs(i, lambda sb: gather_wait(slot, sb))

    @pl.when(i + 1 < n_items)
    def _next_rows():
        @pl.when(s * GATHER_SHARE < nsub_ref[i + 1] * MOE_SUB)
        def _():
            gather_rows(i + 1, 1 - slot, s * GATHER_SHARE, GATHER_SHARE)

    kc = xs_ref.shape[2] // MOE_GU_KCHUNKS

    @pl.when(s < n_gu)
    def _gate_up():
        def prep():
            for c, ref in enumerate(wgu_bf_refs):
                ref[:, 0:ft2] = wgu_a_ref[0, 0, c * kc:(c + 1) * kc, :].astype(BF16)
                ref[:, ft2:2 * ft2] = wgu_b_ref[0, 0, c * kc:(c + 1) * kc, :].astype(BF16)

        def one(sb):
            rows = sub_rows(sb)
            x = xs_ref[slot, rows, :].astype(BF16)
            hgu = None
            for c, ref in enumerate(wgu_bf_refs):
                term = _dot(x[:, c * kc:(c + 1) * kc], ref[...])
                hgu = term if hgu is None else hgu + term
            act = _swiglu_pair(hgu[:, 0:ft2] + bgu_a_ref[0, 0], hgu[:, ft2:2 * ft2] + bgu_b_ref[0, 0])
            act_ref[s, rows, :] = act.astype(BF16)

        _for_each_sub(nsub, prep, one)

    def out_copy(osl, n, sb):
        dst0 = pl.multiple_of(row0 + sb * MOE_SUB, MOE_SUB)
        return pltpu.make_async_copy(
            ot_ref.at[osl, pl.ds(sb * MOE_SUB, MOE_SUB)],
            y_hbm.at[pl.ds(dst0, MOE_SUB), pl.ds(pl.multiple_of(n * MOE_TILE, MOE_TILE), MOE_TILE)],
            osem.at[osl])

    @pl.when(s >= n_gu)
    def _down():
        n = s - n_gu
        osl = n % 2

        @pl.when(n >= 2)
        def _():
            for_item_subs(i, lambda sb: out_copy(osl, 0, sb).wait())

        half = wdn_ref.shape[2] // 2

        def prep():
            for f in range(n_gu):
                for c in range(MOE_TILE // LANES):
                    cols = slice(c * LANES, (c + 1) * LANES)
                    il_refs[f][c, pl.ds(0, MOE_FT, stride=2), :] = wdn_ref[0, 0, f * MOE_FT:(f + 1) * MOE_FT, cols]
                    il_refs[f][c, pl.ds(1, MOE_FT, stride=2), :] = (
                        wdn_ref[0, 0, half + f * MOE_FT:half + (f + 1) * MOE_FT, cols])
                    wdn_bf_refs[f][:, cols] = il_refs[f][c].astype(BF16)

        def one(sb):
            rows = sub_rows(sb)
            acc = None
            for f in range(n_gu):
                term = _dot(act_ref[f, rows, :], wdn_bf_refs[f][...])
                acc = term if acc is None else acc + term
            ot_ref[osl, rows, :] = acc + bdn_ref[0, 0]

        _for_each_sub(nsub, prep, one)
        for_item_subs(i, lambda sb: out_copy(osl, n, sb).start())

        @pl.when(s == n_steps - 1)
        def _():
            for_item_subs(i, lambda sb: out_copy(1 - osl, 0, sb).wait())
            for_item_subs(i, lambda sb: out_copy(osl, 0, sb).wait())


def _moe_experts(h, y_init, plan, w_gu, b_gu, w_dn, b_dn, layer):
    n_slots, d = y_init.shape
    _, n_e, _, two_f = w_gu.shape
    d_ff = two_f // 2
    item_e, item_row0, item_nsub, slot_tok = plan
    n_gu = (d_ff // 2) // MOE_FT
    n_dn = d // MOE_TILE
    n_steps = n_gu + n_dn
    ft2 = 2 * MOE_FT
    assert w_dn.shape[2] == n_gu * MOE_TILE and MOE_ITEM_ROWS // GATHER_SHARE <= n_steps
    slot_tok = jnp.concatenate([slot_tok, jnp.zeros((MOE_ITEM_ROWS,), jnp.int32)])

    def gu_map(half):
        return lambda i, s, ie, r0, ns, st: (layer, ie[i], 0, half * n_gu + jnp.minimum(s, n_gu - 1))

    dn_map = lambda i, s, ie, r0, ns, st: (layer, ie[i], 0, jnp.maximum(s - n_gu, 0))

    grid_spec = pltpu.PrefetchScalarGridSpec(
        num_scalar_prefetch=4,
        grid=(jnp.sum((item_nsub > 0).astype(jnp.int32)), n_steps),
        in_specs=[
            pl.BlockSpec(memory_space=pl.ANY),
            pl.BlockSpec(memory_space=pl.ANY),
            pl.BlockSpec((1, 1, d, ft2), gu_map(0)),
            pl.BlockSpec((1, 1, d, ft2), gu_map(1)),
            pl.BlockSpec((1, 1, 1, ft2), gu_map(0)),
            pl.BlockSpec((1, 1, 1, ft2), gu_map(1)),
            pl.BlockSpec((1, 1, d_ff, MOE_TILE), dn_map),
            pl.BlockSpec((1, 1, 1, MOE_TILE), dn_map),
        ],
        out_specs=pl.BlockSpec(memory_space=pl.ANY),
        scratch_shapes=[
            pltpu.VMEM((2, MOE_ITEM_ROWS, d), F32),
            pltpu.VMEM((n_gu, MOE_ITEM_ROWS, MOE_TILE), BF16),
            pltpu.VMEM((2, MOE_ITEM_ROWS, MOE_TILE), F32),
        ] + [
            pltpu.VMEM((d // MOE_GU_KCHUNKS, 2 * ft2), BF16)
            for _ in range(MOE_GU_KCHUNKS)
        ] + [
            pltpu.VMEM((MOE_TILE // LANES, MOE_TILE, LANES), F32)
            for _ in range(n_gu)
        ] + [
            pltpu.VMEM((MOE_TILE, MOE_TILE), BF16)
            for _ in range(n_gu)
        ] + [
            pltpu.SemaphoreType.DMA((2,)),
            pltpu.SemaphoreType.DMA((2,)),
        ],
    )
    depth = w_gu.shape[0]
    return pl.pallas_call(
        functools.partial(_moe_kernel, n_gu=n_gu),
        grid_spec=grid_spec,
        out_shape=jax.ShapeDtypeStruct((n_slots, d), F32),
        input_output_aliases={5: 0},
        compiler_params=_cparams(("arbitrary", "arbitrary")),
        name="moe_experts",
    )(item_e, item_row0, item_nsub, slot_tok, h, y_init, w_gu, w_gu,
      b_gu.reshape(depth, n_e, 1, two_f), b_gu.reshape(depth, n_e, 1, two_f),
      w_dn, b_dn.reshape(depth, n_e, 1, d))


def _moe_combine_kernel(pos_ref, x_ref, g_ref, gate_ref, y_hbm, o_ref, ybuf_ref, sem):
    i = pl.program_id(0)
    n = pl.num_programs(0)
    tt = x_ref.shape[0]

    def issue(step):
        sl = step % 2

        def body(u, carry):
            for q in range(DMA_UNROLL // TOP_K):
                t = u * (DMA_UNROLL // TOP_K) + q
                for k in range(TOP_K):
                    p = pos_ref[(step * tt + t) * TOP_K + k]
                    pltpu.make_async_copy(y_hbm.at[p], ybuf_ref.at[sl, k, t], sem.at[sl]).start()
            return carry

        lax.fori_loop(0, tt * TOP_K // DMA_UNROLL, body, 0)

    @pl.when(i == 0)
    def _():
        issue(i)

    @pl.when(i + 1 < n)
    def _():
        issue(i + 1)

    slot = i % 2
    pltpu.make_async_copy(ybuf_ref.at[slot], ybuf_ref.at[slot], sem.at[slot]).wait()
    gates = g_ref[...]
    tot = None
    for k in range(TOP_K):
        term = gates[:, k:k + 1] * ybuf_ref[slot, k]
        tot = term if tot is None else tot + term
    o_ref[...] = x_ref[...] + gate_ref[0] * tot


def _moe_combine(x, y, pos, gates, gate_mod):
    b, s, d = x.shape
    tt = COMBINE_TOKENS
    n_tok = b * s
    grid_spec = pltpu.PrefetchScalarGridSpec(
        num_scalar_prefetch=1,
        grid=(n_tok // tt,),
        in_specs=[
            pl.BlockSpec((tt, d), lambda i, pos_r: (i, 0)),
            pl.BlockSpec((tt, TOP_K), lambda i, pos_r: (i, 0)),
            pl.BlockSpec((1, 1, d), lambda i, pos_r: (i * tt // s, 0, 0)),
            pl.BlockSpec(memory_space=pl.ANY),
        ],
        out_specs=pl.BlockSpec((tt, d), lambda i, pos_r: (i, 0)),
        scratch_shapes=[pltpu.VMEM((2, TOP_K, tt, d), F32), pltpu.SemaphoreType.DMA((2,))],
    )
    out = pl.pallas_call(
        _moe_combine_kernel,
        grid_spec=grid_spec,
        out_shape=jax.ShapeDtypeStruct((n_tok, d), F32),
        compiler_params=_cparams(("arbitrary",)),
        name="moe_combine",
    )(pos, x.reshape(n_tok, d), gates.reshape(n_tok, TOP_K), gate_mod, y)
    return out.reshape(b, s, d)


def _moe_ffn(x, g, scale, shift, gate_mod, rw, rb, w_gu, b_gu, w_dn, b_dn, layer, y_buf):
    b, s, d = x.shape
    h, top_idx, gates = _norm_router(x, g, scale, shift, rw, rb)
    item_e, item_row0, item_nsub, slot_tok, pos = _moe_plan(top_idx.reshape(b * s, TOP_K), b * s)
    if y_buf is None:
        y_buf = jnp.zeros((slot_tok.shape[0], d), F32)
    y = _moe_experts(h.reshape(b * s, d), y_buf, (item_e, item_row0, item_nsub, slot_tok),
                     w_gu, b_gu, w_dn, b_dn, layer)
    return _moe_combine(x, y, pos, gates, gate_mod), y


def _rope_lane_tables(seq):
    inv_freq = ROPE_THETA ** (-jnp.arange(0, HEAD_DIM, 2, dtype=F32) / HEAD_DIM)
    ang = jnp.arange(seq, dtype=F32)[:, None] * inv_freq[None, :]
    cos, sin = jnp.cos(ang), jnp.sin(ang)
    reps = LANES // HEAD_DIM
    return (jnp.tile(jnp.concatenate([cos, cos], axis=-1), (1, reps)),
            jnp.tile(jnp.concatenate([-sin, sin], axis=-1), (1, reps)))


def _col_modes(kinds, tn, width):
    modes = []
    for n_cols, mode in kinds:
        assert n_cols % tn == 0
        modes += [mode] * (n_cols // tn)
    assert len(modes) * tn == width
    return jnp.asarray(modes, jnp.int32)


PLAIN, ROPE, SCALE, ROPE_SCALE = 0, 1, 2, 3


def kernel(x, c, ada_w, ada_b, norm_g, final_g, ab_w_in, ab_w_out, na_rpb, diff_lam_q1, diff_lam_k1,
           diff_lam_q2, diff_lam_k2, diff_subln_g, dil_w_in, dil_w_out, moe_router_w, moe_router_b,
           moe_w_gate_up, moe_b_gate_up, moe_w_down, moe_b_down):
    b, s, d = x.shape
    depth = ada_w.shape[0]
    tn = PROJ_TN
    cos_t, sin_t = _rope_lane_tables(s)
    y_buf = None

    c_pad = jnp.zeros((8, d), F32).at[:b].set(c)
    mod = _adaln(c_pad, ada_w.reshape(depth * 2, d, 3 * d), ada_b.reshape(depth * 2, 3 * d))
    mod = mod[:, :b].reshape(depth, 2, b, 3, 1, d)

    na_w = NA_HEADS * HEAD_DIM
    df_w = DIFF_HEADS * 2 * HEAD_DIM
    dil_w = DIL_HEADS * HEAD_DIM

    for l in range(depth):
        shift, scale, gate = (mod[l, 0, :, t] for t in range(3))
        h = _norm_mod(x, norm_g[l, 0], scale, shift, BF16)
        i = l // 2
        if l % 2 == 0:
            lambda_init = 0.8 - 0.6 * math.exp(-0.3 * l)
            modes = _col_modes([(na_w, SCALE), (2 * na_w, PLAIN), (df_w, ROPE_SCALE), (df_w, ROPE),
                                (df_w, PLAIN)], tn, ab_w_in.shape[2])
            proj = _proj(h, ab_w_in[i], modes, cos_t, sin_t, ab_w_in.shape[2], tn=tn)
            proj = proj.reshape(b, s, ab_w_in.shape[2])
            o_na = _na_attention(proj, _na_bias_table(na_rpb[i], s // GRID_W))
            o_df = _diff_attention(proj, diff_lam_q1[i], diff_lam_k1[i], diff_lam_q2[i], diff_lam_k2[i],
                                   diff_subln_g[i], lambda_init, col0=3 * na_w // LANES)
            o = jnp.concatenate([o_na, o_df], axis=-1)
            x = _outproj(o, ab_w_out[i], x, gate)
        else:
            modes = _col_modes([(dil_w, ROPE_SCALE), (dil_w, ROPE), (dil_w, PLAIN)] * len(DIL_CONFIGS),
                               tn, dil_w_in.shape[2])
            outs, lses = [], []
            for gi, (window, dil) in enumerate(DIL_CONFIGS):
                proj = _proj(h, dil_w_in[i], modes, cos_t, sin_t, 3 * dil_w, jbase=gi * (3 * dil_w // tn),
                             dil=dil, tn=tn)
                o_g, lse_g = _dil_attention(proj, window, dil)
                outs.append(o_g)
                lses.append(lse_g)
            o = _dil_mix(outs, lses, [dil for _, dil in DIL_CONFIGS])
            x = _outproj(o, dil_w_out[i], x, gate)
        shift, scale, gate = (mod[l, 1, :, t] for t in range(3))
        x, y_buf = _moe_ffn(x, norm_g[l, 1], scale, shift, gate, moe_router_w[l], moe_router_b[l],
                            moe_w_gate_up, moe_b_gate_up, moe_w_down, moe_b_down, l, y_buf)
    return _final_norm(x, final_g)
```

```python
import functools
import math

import jax
import jax.numpy as jnp
from jax import lax
from jax.experimental import pallas as pl
from jax.experimental.pallas import tpu as pltpu

F32 = jnp.float32
BF16 = jnp.bfloat16

HEAD_DIM = 64
LANES = 128
GRID_W = 64
NA_HEADS = 16
NA_ROWS = 8
NA_COLS = 16
DIFF_HEADS = 8
DIL_HEADS = 32
DIL_CONFIGS = ((128, 1), (512, 4), (2048, 16))
ROPE_THETA = 10000.0
N_EXPERTS = 32
TOP_K = 4
SWIGLU_LIMIT = 7.0
SWIGLU_ALPHA = 1.702
RMS_EPS = 1e-5

PROJ_MCHUNKS = 4
DIL_QBLOCK = 128
DIL_BLOCK_UNROLL = 4
NA_ROW_UNROLL = 4
MOE_SUB = 256
MOE_NSUB = 5
MOE_ITEM_ROWS = MOE_NSUB * MOE_SUB
MOE_FT = 128
MOE_TILE = 2 * MOE_FT
MOE_DN_TILE = 512
PROJ_TN = 1024
MOE_GU_KCHUNKS = 4
DMA_UNROLL = 8
COMBINE_TOKENS = 128
VMEM_LIMIT = 56 * 1024 * 1024


def _cparams(sem, vmem=VMEM_LIMIT):
    return pltpu.CompilerParams(dimension_semantics=sem, vmem_limit_bytes=vmem)


def _dot(a, b):
    return jnp.dot(a, b, preferred_element_type=F32)


def _dot_nt(a, b):
    return lax.dot_general(a, b, (((1,), (1,)), ((), ())), preferred_element_type=F32)


def _split_bf16(a):
    hi = a.astype(BF16)
    lo = (a - hi.astype(F32)).astype(BF16)
    return hi, lo


def _dot3(a, b):
    ah, al = _split_bf16(a)
    bh, bl = _split_bf16(b)
    return _dot(ah, bh) + (_dot(ah, bl) + _dot(al, bh))


def _adaln_kernel(c_ref, w_ref, b_ref, o_ref):
    c = c_ref[...]
    c_act = c * jax.nn.sigmoid(c)
    o_ref[0] = _dot3(c_act, w_ref[0]) + b_ref[0]


def _adaln(c_pad, w, b, tn=768):
    n_mod, d, n = w.shape
    return pl.pallas_call(
        _adaln_kernel,
        grid=(n_mod, n // tn),
        in_specs=[
            pl.BlockSpec(c_pad.shape, lambda l, j: (0, 0)),
            pl.BlockSpec((1, d, tn), lambda l, j: (l, 0, j)),
            pl.BlockSpec((1, 1, tn), lambda l, j: (l, 0, j)),
        ],
        out_specs=pl.BlockSpec((1, c_pad.shape[0], tn), lambda l, j: (l, 0, j)),
        out_shape=jax.ShapeDtypeStruct((n_mod, c_pad.shape[0], n), F32),
        compiler_params=_cparams(("arbitrary", "arbitrary")),
        name="adaln",
    )(c_pad, w, b.reshape(n_mod, 1, n))


def _rms(x, g):
    return x * lax.rsqrt(jnp.mean(x * x, axis=-1, keepdims=True) + RMS_EPS) * g


def _norm_mod_kernel(x_ref, g_ref, scale_ref, shift_ref, o_ref):
    h = _rms(x_ref[0], g_ref[...]) * (1.0 + scale_ref[0]) + shift_ref[0]
    o_ref[0] = h.astype(o_ref.dtype)


def _norm_mod(x, g, scale, shift, out_dtype, tm=512):
    b, s, d = x.shape
    return pl.pallas_call(
        _norm_mod_kernel,
        grid=(b, s // tm),
        in_specs=[
            pl.BlockSpec((1, tm, d), lambda bi, i: (bi, i, 0)),
            pl.BlockSpec((1, d), lambda bi, i: (0, 0)),
            pl.BlockSpec((1, 1, d), lambda bi, i: (bi, 0, 0)),
            pl.BlockSpec((1, 1, d), lambda bi, i: (bi, 0, 0)),
        ],
        out_specs=pl.BlockSpec((1, tm, d), lambda bi, i: (bi, i, 0)),
        out_shape=jax.ShapeDtypeStruct((b, s, d), out_dtype),
        compiler_params=_cparams(("arbitrary", "arbitrary")),
        name="norm_mod",
    )(x, g.reshape(1, d), scale, shift)


def _final_norm_kernel(x_ref, g_ref, o_ref):
    o_ref[0] = _rms(x_ref[0], g_ref[...])


def _final_norm(x, g, tm=512):
    b, s, d = x.shape
    return pl.pallas_call(
        _final_norm_kernel,
        grid=(b, s // tm),
        in_specs=[
            pl.BlockSpec((1, tm, d), lambda bi, i: (bi, i, 0)),
            pl.BlockSpec((1, d), lambda bi, i: (0, 0)),
        ],
        out_specs=pl.BlockSpec((1, tm, d), lambda bi, i: (bi, i, 0)),
        out_shape=jax.ShapeDtypeStruct((b, s, d), F32),
        compiler_params=_cparams(("arbitrary", "arbitrary")),
        name="final_norm",
    )(x, g.reshape(1, d))


def _proj_kernel(mode_ref, a_ref, w_ref, cos_ref, sin_ref, o_ref, wbf_ref, stage_ref, *, jbase, dil):
    j = pl.program_id(0)

    @pl.when((pl.program_id(1) == 0) & (pl.program_id(2) == 0))
    def _():
        wbf_ref[...] = w_ref[...].astype(BF16)

    mode = mode_ref[jbase + j]
    tm = a_ref.shape[1]
    tn = wbf_ref.shape[1]
    cm = tm // PROJ_MCHUNKS

    def emit(mi, ci, val):
        cols = slice(ci * LANES, (ci + 1) * LANES)
        if dil == 1:
            o_ref[0, 0, mi * cm:(mi + 1) * cm, cols] = val.astype(o_ref.dtype)
        else:
            stage_ref[ci, mi * cm:(mi + 1) * cm, :] = val
            n = cm // dil
            for r in range(dil):
                o_ref[0, r, mi * n:(mi + 1) * n, cols] = (
                    stage_ref[ci, pl.ds(mi * cm + r, n, stride=dil), :].astype(o_ref.dtype))

    @pl.when(mode == 0)
    def _():
        for mi in range(PROJ_MCHUNKS):
            acc = _dot(a_ref[0, mi * cm:(mi + 1) * cm, :], wbf_ref[...])
            for ci in range(tn // LANES):
                emit(mi, ci, acc[:, ci * LANES:(ci + 1) * LANES])

    @pl.when(mode != 0)
    def _():
        rope_on = (mode & 1).astype(F32)
        qscale = jnp.where((mode & 2) == 2, HEAD_DIM ** -0.5, 1.0).astype(F32)
        lane = lax.broadcasted_iota(jnp.int32, (1, LANES), 1)
        first_half = (lane % HEAD_DIM) < (HEAD_DIM // 2)
        for mi in range(PROJ_MCHUNKS):
            rows = slice(mi * cm, (mi + 1) * cm)
            acc = _dot(a_ref[0, rows, :], wbf_ref[...])
            cos = (cos_ref[rows, :] * rope_on + (1.0 - rope_on)) * qscale
            sin = sin_ref[rows, :] * (rope_on * qscale)
            for ci in range(tn // LANES):
                x = acc[:, ci * LANES:(ci + 1) * LANES]
                partner = jnp.where(first_half,
                                    pltpu.roll(x, LANES - HEAD_DIM // 2, 1),
                                    pltpu.roll(x, HEAD_DIM // 2, 1))
                emit(mi, ci, x * cos + partner * sin)


def _proj(a, w, modes, cos_t, sin_t, n_out, jbase=0, dil=1, tm=1024, tn=PROJ_TN):
    b, s, k = a.shape
    grid_spec = pltpu.PrefetchScalarGridSpec(
        num_scalar_prefetch=1,
        grid=(n_out // tn, b, s // tm),
        in_specs=[
            pl.BlockSpec((1, tm, k), lambda j, bi, i, m: (bi, i, 0)),
            pl.BlockSpec((k, tn), lambda j, bi, i, m: (0, jbase + j)),
            pl.BlockSpec((tm, LANES), lambda j, bi, i, m: (i, 0)),
            pl.BlockSpec((tm, LANES), lambda j, bi, i, m: (i, 0)),
        ],
        out_specs=pl.BlockSpec((1, dil, tm // dil, tn), lambda j, bi, i, m: (bi, 0, i, j)),
        scratch_shapes=[pltpu.VMEM((k, tn), BF16), pltpu.VMEM((tn // LANES, tm, LANES), F32)],
    )
    return pl.pallas_call(
        functools.partial(_proj_kernel, jbase=jbase, dil=dil),
        grid_spec=grid_spec,
        out_shape=jax.ShapeDtypeStruct((b, dil, s // dil, n_out), BF16),
        compiler_params=_cparams(("arbitrary", "arbitrary", "arbitrary")),
        name="proj",
    )(modes, a, w, cos_t, sin_t)


def _outproj_kernel(a_ref, w_ref, x_ref, gate_ref, o_ref, wbf_ref):
    @pl.when((pl.program_id(1) == 0) & (pl.program_id(2) == 0))
    def _():
        wbf_ref[...] = w_ref[...].astype(BF16)

    o_ref[0] = x_ref[0] + gate_ref[0] * _dot(a_ref[0], wbf_ref[...])


def _outproj(a, w, x, gate, tm=1024, tn=512):
    b, s, k = a.shape
    n = w.shape[1]
    return pl.pallas_call(
        _outproj_kernel,
        grid=(n // tn, b, s // tm),
        in_specs=[
            pl.BlockSpec((1, tm, k), lambda j, bi, i: (bi, i, 0)),
            pl.BlockSpec((k, tn), lambda j, bi, i: (0, j)),
            pl.BlockSpec((1, tm, tn), lambda j, bi, i: (bi, i, j)),
            pl.BlockSpec((1, 1, tn), lambda j, bi, i: (bi, 0, j)),
        ],
        out_specs=pl.BlockSpec((1, tm, tn), lambda j, bi, i: (bi, i, j)),
        out_shape=jax.ShapeDtypeStruct((b, s, n), F32),
        scratch_shapes=[pltpu.VMEM((k, tn), BF16)],
        compiler_params=_cparams(("arbitrary", "arbitrary", "arbitrary")),
        name="outproj",
    )(a, w, x, gate)


def _columns(cols):
    m = cols[0].shape[0]
    lane = lax.broadcasted_iota(jnp.int32, (m, len(cols)), 1)
    out = jnp.broadcast_to(cols[0], (m, len(cols)))
    for k in range(1, len(cols)):
        out = jnp.where(lane == k, cols[k], out)
    return out


def _head_masks():
    lane = lax.broadcasted_iota(jnp.int32, (1, LANES), 1)
    return lane < HEAD_DIM, lane >= HEAD_DIM


def _na_kernel(q_ref, k_ref, v_ref, bias_ref, o_ref, *, rows):
    kr = min(NA_ROWS, rows)
    masks = _head_masks()

    def body(t, carry):
        r_list = [t * NA_ROW_UNROLL + u for u in range(NA_ROW_UNROLL)]
        scores, values = [], []
        for r in r_list:
            rs = jnp.clip(r - kr // 2, 0, rows - kr)
            var = r - rs
            q = q_ref[0, pl.ds(pl.multiple_of(r * GRID_W, GRID_W), GRID_W), :]
            k = k_ref[0, pl.ds(pl.multiple_of(rs * GRID_W, GRID_W), kr * GRID_W), :]
            values.append(v_ref[0, pl.ds(pl.multiple_of(rs * GRID_W, GRID_W), kr * GRID_W), :])
            for h in range(2):
                qh = jnp.where(masks[h], q, jnp.zeros_like(q))
                scores.append(_dot_nt(qh, k) + bias_ref[h, var])
        probs, dens = [], []
        for sc in scores:
            e = jnp.exp(sc - jnp.max(sc, axis=-1, keepdims=True))
            dens.append(jnp.sum(e, axis=-1, keepdims=True))
            probs.append(e.astype(BF16))
        for u, r in enumerate(r_list):
            outs = [_dot(probs[2 * u + h], values[u]) / dens[2 * u + h] for h in range(2)]
            o = jnp.where(masks[0], outs[0], outs[1])
            o_ref[0, pl.ds(pl.multiple_of(r * GRID_W, GRID_W), GRID_W), :] = o.astype(o_ref.dtype)
        return carry

    assert rows % NA_ROW_UNROLL == 0
    lax.fori_loop(0, rows // NA_ROW_UNROLL, body, 0)


def _na_bias_table(rpb, rows):
    kr = min(NA_ROWS, rows)
    c = jnp.arange(GRID_W)
    kc = jnp.arange(GRID_W)
    cs = jnp.clip(c - NA_COLS // 2, 0, GRID_W - NA_COLS)
    valid = (kc[None, :] >= cs[:, None]) & (kc[None, :] < cs[:, None] + NA_COLS)
    coff = jnp.clip(kc[None, :] - c[:, None] + (NA_COLS - 1), 0, 2 * NA_COLS - 2)
    rows_v = jnp.stack([rpb[:, NA_ROWS - 1 - v:NA_ROWS - 1 - v + kr, :] for v in range(kr)], axis=1)
    onehot = ((coff[None] == jnp.arange(2 * NA_COLS - 1)[:, None, None]) & valid[None]).astype(F32)
    tbl = jnp.einsum('hvio,ock->hvcik', rows_v.astype(F32), onehot, precision=lax.Precision.HIGHEST)
    tbl = jnp.where(valid[None, None, :, None, :], tbl, -jnp.inf)
    return tbl.reshape(rpb.shape[0], kr, GRID_W, kr * GRID_W)


def _na_attention(proj, bias_tbl):
    b, s, _ = proj.shape
    rows = s // GRID_W
    npair = NA_HEADS // 2
    kr = bias_tbl.shape[1]
    blk = lambda off: pl.BlockSpec((1, s, LANES), lambda bi, c: (bi, 0, off + c))
    return pl.pallas_call(
        functools.partial(_na_kernel, rows=rows),
        grid=(b, npair),
        in_specs=[
            blk(0), blk(npair), blk(2 * npair),
            pl.BlockSpec((2, kr, GRID_W, kr * GRID_W), lambda bi, c: (c, 0, 0, 0)),
        ],
        out_specs=pl.BlockSpec((1, s, LANES), lambda bi, c: (bi, 0, c)),
        out_shape=jax.ShapeDtypeStruct((b, s, NA_HEADS * HEAD_DIM), BF16),
        compiler_params=_cparams(("arbitrary", "arbitrary")),
        name="na_attn",
    )(proj, proj, proj, bias_tbl)


def _diff_kernel(lam_ref, q_ref, k_ref, v_ref, g_ref, o_ref, *, tk, lambda_init):
    masks = _head_masks()
    q = q_ref[0]
    tq = q.shape[0]
    s = k_ref.shape[1]
    qs = [jnp.where(masks[m], q, jnp.zeros_like(q)) for m in range(2)]

    def body(j, carry):
        k = k_ref[0, pl.ds(pl.multiple_of(j * tk, tk), tk), :]
        v = v_ref[0, pl.ds(pl.multiple_of(j * tk, tk), tk), :]
        scores = [_dot_nt(qs[m], k) for m in range(2)]
        stats = []
        for m in range(2):
            m_old, l_old = carry[3 * m], carry[3 * m + 1]
            m_new = jnp.maximum(m_old, jnp.max(scores[m], axis=-1, keepdims=True))
            alpha = jnp.exp(m_old - m_new)
            e = jnp.exp(scores[m] - m_new)
            l_new = alpha * l_old + jnp.sum(e, axis=-1, keepdims=True)
            stats.append((m_new, l_new, alpha, e.astype(BF16)))
        new = []
        for m in range(2):
            m_new, l_new, alpha, p = stats[m]
            new += [m_new, l_new, alpha * carry[3 * m + 2] + _dot(p, v)]
        return tuple(new)

    init = []
    for m in range(2):
        init += [jnp.full((tq, 1), -jnp.inf, F32), jnp.zeros((tq, 1), F32), jnp.zeros((tq, LANES), F32)]
    res = lax.fori_loop(0, s // tk, body, tuple(init))
    lam = lam_ref[0]
    o = res[2] / res[1] - lam * (res[5] / res[4])
    o_ref[0] = (_rms(o, g_ref[...]) * (1.0 - lambda_init)).astype(o_ref.dtype)


def _diff_lambda_kernel(lq1_ref, lk1_ref, lq2_ref, lk2_ref, o_ref, *, lambda_init):
    s1 = jnp.sum(lq1_ref[...] * lk1_ref[...], axis=-1, keepdims=True)
    s2 = jnp.sum(lq2_ref[...] * lk2_ref[...], axis=-1, keepdims=True)
    o_ref[...] = jnp.exp(s1) - jnp.exp(s2) + lambda_init


def _diff_attention(proj, lq1, lk1, lq2, lk2, subln_g, lambda_init, col0, tq=512, tk=512):
    b, s, _ = proj.shape
    nh = DIFF_HEADS
    vec = lambda a: a.reshape(1, HEAD_DIM).astype(F32)
    lam = pl.pallas_call(
        functools.partial(_diff_lambda_kernel, lambda_init=lambda_init),
        out_shape=jax.ShapeDtypeStruct((1, 1), F32),
        name="diff_lambda",
    )(vec(lq1), vec(lk1), vec(lq2), vec(lk2)).reshape(1)
    grid_spec = pltpu.PrefetchScalarGridSpec(
        num_scalar_prefetch=1,
        grid=(b, nh, s // tq),
        in_specs=[
            pl.BlockSpec((1, tq, LANES), lambda bi, h, i, lam_r: (bi, i, col0 + h)),
            pl.BlockSpec((1, s, LANES), lambda bi, h, i, lam_r: (bi, 0, col0 + nh + h)),
            pl.BlockSpec((1, s, LANES), lambda bi, h, i, lam_r: (bi, 0, col0 + 2 * nh + h)),
            pl.BlockSpec((1, LANES), lambda bi, h, i, lam_r: (0, 0)),
        ],
        out_specs=pl.BlockSpec((1, tq, LANES), lambda bi, h, i, lam_r: (bi, i, h)),
    )
    return pl.pallas_call(
        functools.partial(_diff_kernel, tk=tk, lambda_init=lambda_init),
        grid_spec=grid_spec,
        out_shape=jax.ShapeDtypeStruct((b, s, nh * LANES), BF16),
        compiler_params=_cparams(("arbitrary", "arbitrary", "arbitrary")),
        name="diff_attn",
    )(lam, proj, proj, proj, subln_g.reshape(1, LANES).astype(F32))


def _dil_kernel(q_ref, k_ref, v_ref, o_ref, lse_ref, *, n_side):
    masks = _head_masks()
    length = q_ref.shape[2]
    tq = DIL_QBLOCK
    tk = tq + 2 * n_side
    rel0 = (lax.broadcasted_iota(jnp.int32, (tq, tk), 1)
            - lax.broadcasted_iota(jnp.int32, (tq, tk), 0))

    nq = length // tq
    planes = q_ref.shape[1]

    def body(t, carry):
        where, scores, values = [], [], []
        for u in range(DIL_BLOCK_UNROLL):
            bi = t * DIL_BLOCK_UNROLL + u
            p = bi // nq
            a0 = pl.multiple_of((bi % nq) * tq, tq)
            ks = pl.multiple_of(jnp.clip(a0 - n_side, 0, length - tk), n_side)
            q = q_ref[0, p, pl.ds(a0, tq), :]
            k = k_ref[0, p, pl.ds(ks, tk), :]
            values.append(v_ref[0, p, pl.ds(ks, tk), :])
            where.append((p, a0))
            rel = rel0 + (ks - a0)
            band = (rel >= -n_side) & (rel <= n_side)
            for h in range(2):
                qh = jnp.where(masks[h], q, jnp.zeros_like(q))
                scores.append(jnp.where(band, _dot_nt(qh, k), -jnp.inf))
        probs, dens, lses = [], [], []
        for sc in scores:
            m = jnp.max(sc, axis=-1, keepdims=True)
            e = jnp.exp(sc - m)
            den = jnp.sum(e, axis=-1, keepdims=True)
            probs.append(e.astype(BF16))
            dens.append(den)
            lses.append(m + jnp.log(den))
        for u, (p, a0) in enumerate(where):
            outs = [_dot(probs[2 * u + h], values[u]) / dens[2 * u + h] for h in range(2)]
            o_ref[0, p, pl.ds(a0, tq), :] = jnp.where(masks[0], outs[0], outs[1]).astype(o_ref.dtype)
            lse_ref[0, 0, p, pl.ds(a0, tq), :] = _columns(lses[2 * u:2 * u + 2])
        return carry

    assert (planes * nq) % DIL_BLOCK_UNROLL == 0
    lax.fori_loop(0, planes * nq // DIL_BLOCK_UNROLL, body, 0)


def _dil_attention(proj, window, dil):
    b, _, length, _ = proj.shape
    n_side = (window // 2) // dil
    npair = DIL_HEADS // 2
    cw = DIL_HEADS * HEAD_DIM
    rp = min(dil, max(1, DIL_BLOCK_UNROLL * DIL_QBLOCK // length))
    blk = lambda off: pl.BlockSpec((1, rp, length, LANES), lambda bi, c, r: (bi, r, 0, off + c))
    return pl.pallas_call(
        functools.partial(_dil_kernel, n_side=n_side),
        grid=(b, npair, dil // rp),
        in_specs=[blk(0), blk(npair), blk(2 * npair)],
        out_specs=[
            pl.BlockSpec((1, rp, length, LANES), lambda bi, c, r: (bi, r, 0, c)),
            pl.BlockSpec((1, 1, rp, length, 2), lambda bi, c, r: (c, bi, r, 0, 0)),
        ],
        out_shape=[
            jax.ShapeDtypeStruct((b, dil, length, cw), BF16),
            jax.ShapeDtypeStruct((npair, b, dil, length, 2), F32),
        ],
        compiler_params=_cparams(("arbitrary", "arbitrary", "arbitrary")),
        name="dil_attn",
    )(proj, proj, proj)


def _dil_mix_kernel(*refs, dils):
    ng = len(dils)
    o_refs, l_refs = refs[:ng], refs[ng:2 * ng]
    out_ref, so_ref, sl_ref = refs[2 * ng:]
    masks = _head_masks()
    tt = out_ref.shape[1]

    def widen(lse):
        return jnp.where(masks[0], lse[:, 0:1], lse[:, 1:2])

    outs, lses = [], []
    for g, dil in enumerate(dils):
        if dil == 1:
            outs.append(o_refs[g][0, 0].astype(F32))
            lses.append(widen(l_refs[g][0, 0, 0]))
        else:
            n = tt // dil
            for r in range(dil):
                so_ref[g, pl.ds(r, n, stride=dil), :] = o_refs[g][0, r].astype(F32)
                sl_ref[g, pl.ds(r, n, stride=dil), :] = widen(l_refs[g][0, 0, r])
            outs.append(so_ref[g])
            lses.append(sl_ref[g])
    mx = lses[0]
    for l in lses[1:]:
        mx = jnp.maximum(mx, l)
    ws = [jnp.exp(l - mx) for l in lses]
    tot = ws[0]
    for w in ws[1:]:
        tot = tot + w
    acc = None
    for w, o in zip(ws, outs):
        term = (w / tot) * o
        acc = term if acc is None else acc + term
    out_ref[0] = acc.astype(out_ref.dtype)


def _dil_mix(outs, lses, dils, tt=1024):
    b, _, _, cw = outs[0].shape
    s = outs[0].shape[1] * outs[0].shape[2]
    npair = cw // LANES
    ng = len(dils)
    o_specs = [pl.BlockSpec((1, d, tt // d, LANES), lambda bi, i, c: (bi, 0, i, c)) for d in dils]
    l_specs = [pl.BlockSpec((1, 1, d, tt // d, 2), lambda bi, i, c: (c, bi, 0, i, 0)) for d in dils]
    return pl.pallas_call(
        functools.partial(_dil_mix_kernel, dils=tuple(dils)),
        grid=(b, s // tt, npair),
        in_specs=o_specs + l_specs,
        out_specs=pl.BlockSpec((1, tt, LANES), lambda bi, i, c: (bi, i, c)),
        out_shape=jax.ShapeDtypeStruct((b, s, cw), BF16),
        scratch_shapes=[pltpu.VMEM((ng, tt, LANES), F32), pltpu.VMEM((ng, tt, LANES), F32)],
        compiler_params=_cparams(("arbitrary", "arbitrary", "arbitrary")),
        name="dil_mix",
    )(*outs, *lses)


def _norm_router_kernel(x_ref, g_ref, scale_ref, shift_ref, rw_ref, rb_ref, h_ref, idx_ref, gate_ref):
    h = _rms(x_ref[0], g_ref[...]) * (1.0 + scale_ref[0]) + shift_ref[0]
    h_ref[0] = h
    logits = _dot3(h, rw_ref[...]) + rb_ref[...]
    n_e = logits.shape[1]
    eidx = lax.broadcasted_iota(jnp.int32, logits.shape, 1).astype(F32)
    vals, idxs = [], []
    for _ in range(TOP_K):
        m = jnp.max(logits, axis=-1, keepdims=True)
        first = jnp.min(jnp.where(logits == m, eidx, float(n_e)), axis=-1, keepdims=True)
        vals.append(m)
        idxs.append(first)
        logits = jnp.where(eidx == first, -jnp.inf, logits)
    top = _columns(vals)
    e = jnp.exp(top - vals[0])
    gate_ref[0] = e / jnp.sum(e, axis=-1, keepdims=True)
    idx_ref[0] = _columns(idxs).astype(jnp.int32)


def _norm_router(x, g, scale, shift, rw, rb, tm=512):
    b, s, d = x.shape
    n_e = rw.shape[1]
    tok = lambda w: pl.BlockSpec((1, tm, w), lambda bi, i: (bi, i, 0))
    return pl.pallas_call(
        _norm_router_kernel,
        grid=(b, s // tm),
        in_specs=[
            tok(d),
            pl.BlockSpec((1, d), lambda bi, i: (0, 0)),
            pl.BlockSpec((1, 1, d), lambda bi, i: (bi, 0, 0)),
            pl.BlockSpec((1, 1, d), lambda bi, i: (bi, 0, 0)),
            pl.BlockSpec((d, n_e), lambda bi, i: (0, 0)),
            pl.BlockSpec((1, n_e), lambda bi, i: (0, 0)),
        ],
        out_specs=[tok(d), tok(TOP_K), tok(TOP_K)],
        out_shape=[
            jax.ShapeDtypeStruct((b, s, d), F32),
            jax.ShapeDtypeStruct((b, s, TOP_K), jnp.int32),
            jax.ShapeDtypeStruct((b, s, TOP_K), F32),
        ],
        compiler_params=_cparams(("arbitrary", "arbitrary")),
        name="norm_router",
    )(x, g.reshape(1, d), scale, shift, rw, rb.reshape(1, n_e))


def _moe_plan(top_idx, n_tok):
    n_assign = n_tok * TOP_K
    flat_e = top_idx.reshape(-1)
    onehot = (flat_e[:, None] == jnp.arange(N_EXPERTS, dtype=jnp.int32)[None, :]).astype(jnp.int32)
    csum = jnp.cumsum(onehot, axis=0)
    counts = csum[-1]
    rank = jnp.take_along_axis(csum, flat_e[:, None], axis=1)[:, 0] - 1
    padded = (counts + MOE_SUB - 1) // MOE_SUB * MOE_SUB
    pstarts = jnp.cumsum(padded) - padded
    dest = pstarts[flat_e] + rank
    n_slots = n_assign + N_EXPERTS * MOE_SUB
    slot_tok = jnp.zeros((n_slots,), jnp.int32).at[dest].set(
        jnp.arange(n_assign, dtype=jnp.int32) // TOP_K)
    n_items = (n_slots + N_EXPERTS * (MOE_ITEM_ROWS - MOE_SUB)) // MOE_ITEM_ROWS
    items_per_e = (padded + MOE_ITEM_ROWS - 1) // MOE_ITEM_ROWS
    item_ends = jnp.cumsum(items_per_e)
    total_items = item_ends[-1]
    i = jnp.arange(n_items, dtype=jnp.int32)
    i_eff = jnp.minimum(i, total_items - 1)
    e_i = jnp.clip(jnp.searchsorted(item_ends, i_eff, side='right'), 0, N_EXPERTS - 1).astype(jnp.int32)
    local = i_eff - (item_ends[e_i] - items_per_e[e_i])
    row0 = pstarts[e_i] + local * MOE_ITEM_ROWS
    nrows = jnp.clip(padded[e_i] - local * MOE_ITEM_ROWS, 0, MOE_ITEM_ROWS)
    nsub = jnp.where(i < total_items, nrows // MOE_SUB, 0)
    return (e_i.astype(jnp.int32), row0.astype(jnp.int32), nsub.astype(jnp.int32),
            slot_tok, dest.astype(jnp.int32))


def _swiglu_pair(h_a, h_b):
    n = h_a.shape[1]
    lane = lax.broadcasted_iota(jnp.int32, (1, n), 1)
    even = (lane % 2) == 0

    def glu_lin(h):
        glu = jnp.minimum(h, SWIGLU_LIMIT)
        glu = glu * jax.nn.sigmoid(SWIGLU_ALPHA * glu)
        lin = jnp.clip(h, -SWIGLU_LIMIT, SWIGLU_LIMIT) + 1.0
        return glu, lin

    glu_a, lin_a = glu_lin(h_a)
    glu_b, lin_b = glu_lin(h_b)
    act_a = glu_a * pltpu.roll(lin_a, n - 1, 1)
    act_b = pltpu.roll(glu_b, 1, 1) * lin_b
    return jnp.where(even, act_a, act_b)


def _for_each_sub(nsub, prep, fn):
    for count in range(1, MOE_NSUB + 1):
        @pl.when(nsub == count)
        def _():
            prep()
            for sb in range(count):
                fn(sb)


def _moe_kernel(ie_ref, row0_ref, nsub_ref, stok_ref,
                h_hbm, y_init_hbm, wgu_a_ref, wgu_b_ref, bgu_ref, wdn_ref, bdn_ref,
                y_hbm,
                xs_ref, xsb_ref, act_ref, ot_ref, *scratch, n_gu, share):
    del y_init_hbm
    wgu_bf_refs = scratch[:MOE_GU_KCHUNKS]
    il_refs = scratch[MOE_GU_KCHUNKS:MOE_GU_KCHUNKS + n_gu]
    wdn_bf_refs = scratch[MOE_GU_KCHUNKS + n_gu:MOE_GU_KCHUNKS + 2 * n_gu]
    xsem, osem = scratch[MOE_GU_KCHUNKS + 2 * n_gu:]
    i = pl.program_id(0)
    s = pl.program_id(1)
    n_items = pl.num_programs(0)
    n_steps = pl.num_programs(1)
    nsub = nsub_ref[i]
    row0 = row0_ref[i]
    e = ie_ref[i]
    ft2 = 2 * MOE_FT

    def sub_rows(sb):
        return slice(sb * MOE_SUB, (sb + 1) * MOE_SUB)

    def gather_rows(item, first, count):
        def body(u, carry):
            for q in range(DMA_UNROLL):
                r = first + u * DMA_UNROLL + q
                tok = stok_ref[row0_ref[item] + r]
                pltpu.make_async_copy(h_hbm.at[tok], xs_ref.at[r], xsem).start()
            return carry

        lax.fori_loop(0, count // DMA_UNROLL, body, 0)

    def for_item_subs(item, fn):
        for sb in range(MOE_NSUB):
            @pl.when(sb < nsub_ref[item])
            def _():
                fn(sb)

    @pl.when(s == 0)
    def _own_rows():
        @pl.when(i == 0)
        def _():
            gather_rows(i, 0, nsub * MOE_SUB)

        def wait(sb):
            rows = xs_ref.at[pl.ds(sb * MOE_SUB, MOE_SUB)]
            pltpu.make_async_copy(rows, rows, xsem).wait()

        def convert(sb):
            xsb_ref[sub_rows(sb), :] = xs_ref[sub_rows(sb), :].astype(BF16)

        for_item_subs(i, wait)
        for_item_subs(i, convert)

    @pl.when(i + 1 < n_items)
    def _next_rows():
        @pl.when(s * share < nsub_ref[i + 1] * MOE_SUB)
        def _():
            gather_rows(i + 1, s * share, share)

    kc = xs_ref.shape[1] // MOE_GU_KCHUNKS

    @pl.when(s < n_gu)
    def _gate_up():
        def prep():
            for c, ref in enumerate(wgu_bf_refs):
                ref[:, 0:ft2] = wgu_a_ref[0, 0, c * kc:(c + 1) * kc, :].astype(BF16)
                ref[:, ft2:2 * ft2] = wgu_b_ref[0, 0, c * kc:(c + 1) * kc, :].astype(BF16)

        def one(sb):
            rows = sub_rows(sb)
            x = xsb_ref[rows, :]
            hgu = None
            for c, ref in enumerate(wgu_bf_refs):
                term = _dot(x[:, c * kc:(c + 1) * kc], ref[...])
                hgu = term if hgu is None else hgu + term
            b_a = bgu_ref[0, e, pl.ds(s, 1), :]
            b_b = bgu_ref[0, e, pl.ds(n_gu + s, 1), :]
            act = _swiglu_pair(hgu[:, 0:ft2] + b_a, hgu[:, ft2:2 * ft2] + b_b)
            act_ref[s, rows, :] = act.astype(BF16)

        _for_each_sub(nsub, prep, one)

    def out_copy(osl, n, sb):
        dst0 = pl.multiple_of(row0 + sb * MOE_SUB, MOE_SUB)
        return pltpu.make_async_copy(
            ot_ref.at[osl, pl.ds(sb * MOE_SUB, MOE_SUB)],
            y_hbm.at[pl.ds(dst0, MOE_SUB), pl.ds(pl.multiple_of(n * MOE_DN_TILE, MOE_DN_TILE), MOE_DN_TILE)],
            osem.at[osl])

    @pl.when(s >= n_gu)
    def _down():
        n = s - n_gu
        osl = n % 2

        @pl.when(n >= 2)
        def _():
            for_item_subs(i, lambda sb: out_copy(osl, 0, sb).wait())

        half = wdn_ref.shape[2] // 2

        def prep():
            for f in range(n_gu):
                for c in range(MOE_DN_TILE // LANES):
                    cols = slice(c * LANES, (c + 1) * LANES)
                    il_refs[f][c, pl.ds(0, MOE_FT, stride=2), :] = wdn_ref[0, 0, f * MOE_FT:(f + 1) * MOE_FT, cols]
                    il_refs[f][c, pl.ds(1, MOE_FT, stride=2), :] = (
                        wdn_ref[0, 0, half + f * MOE_FT:half + (f + 1) * MOE_FT, cols])
                    wdn_bf_refs[f][:, cols] = il_refs[f][c].astype(BF16)

        def one(sb):
            rows = sub_rows(sb)
            acc = None
            for f in range(n_gu):
                term = _dot(act_ref[f, rows, :], wdn_bf_refs[f][...])
                acc = term if acc is None else acc + term
            ot_ref[osl, rows, :] = acc + bdn_ref[0, e, pl.ds(n, 1), :]

        _for_each_sub(nsub, prep, one)
        for_item_subs(i, lambda sb: out_copy(osl, n, sb).start())

        @pl.when(s == n_steps - 1)
        def _():
            for_item_subs(i, lambda sb: out_copy(1 - osl, 0, sb).wait())
            for_item_subs(i, lambda sb: out_copy(osl, 0, sb).wait())


def _moe_experts(h, y_init, plan, w_gu, b_gu, w_dn, b_dn, layer):
    n_slots, d = y_init.shape
    _, n_e, _, two_f = w_gu.shape
    d_ff = two_f // 2
    item_e, item_row0, item_nsub, slot_tok = plan
    n_gu = (d_ff // 2) // MOE_FT
    n_dn = d // MOE_DN_TILE
    n_steps = n_gu + n_dn
    ft2 = 2 * MOE_FT
    assert w_dn.shape[2] == n_gu * MOE_TILE and n_steps >= MOE_NSUB
    share = MOE_SUB // 2 if n_steps >= 2 * MOE_NSUB else MOE_SUB
    slot_tok = jnp.concatenate([slot_tok, jnp.zeros((MOE_ITEM_ROWS,), jnp.int32)])

    def gu_map(half):
        return lambda i, s, ie, r0, ns, st: (layer, ie[i], 0, half * n_gu + jnp.minimum(s, n_gu - 1))

    dn_map = lambda i, s, ie, r0, ns, st: (layer, ie[i], 0, jnp.maximum(s - n_gu, 0))
    layer_map = lambda i, s, ie, r0, ns, st: (layer, 0, 0, 0)

    grid_spec = pltpu.PrefetchScalarGridSpec(
        num_scalar_prefetch=4,
        grid=(jnp.sum((item_nsub > 0).astype(jnp.int32)), n_steps),
        in_specs=[
            pl.BlockSpec(memory_space=pl.ANY),
            pl.BlockSpec(memory_space=pl.ANY),
            pl.BlockSpec((1, 1, d, ft2), gu_map(0)),
            pl.BlockSpec((1, 1, d, ft2), gu_map(1)),
            pl.BlockSpec((1, n_e, two_f // ft2, ft2), layer_map),
            pl.BlockSpec((1, 1, d_ff, MOE_DN_TILE), dn_map),
            pl.BlockSpec((1, n_e, n_dn, MOE_DN_TILE), layer_map),
        ],
        out_specs=pl.BlockSpec(memory_space=pl.ANY),
        scratch_shapes=[
            pltpu.VMEM((MOE_ITEM_ROWS, d), F32),
            pltpu.VMEM((MOE_ITEM_ROWS, d), BF16),
            pltpu.VMEM((n_gu, MOE_ITEM_ROWS, MOE_TILE), BF16),
            pltpu.VMEM((2, MOE_ITEM_ROWS, MOE_DN_TILE), F32),
        ] + [
            pltpu.VMEM((d // MOE_GU_KCHUNKS, 2 * ft2), BF16)
            for _ in range(MOE_GU_KCHUNKS)
        ] + [
            pltpu.VMEM((MOE_DN_TILE // LANES, MOE_TILE, LANES), F32)
            for _ in range(n_gu)
        ] + [
            pltpu.VMEM((MOE_TILE, MOE_DN_TILE), BF16)
            for _ in range(n_gu)
        ] + [
            pltpu.SemaphoreType.DMA(()),
            pltpu.SemaphoreType.DMA((2,)),
        ],
    )
    depth = w_gu.shape[0]
    return pl.pallas_call(
        functools.partial(_moe_kernel, n_gu=n_gu, share=share),
        grid_spec=grid_spec,
        out_shape=jax.ShapeDtypeStruct((n_slots, d), F32),
        input_output_aliases={5: 0},
        compiler_params=_cparams(("arbitrary", "arbitrary")),
        name="moe_experts",
    )(item_e, item_row0, item_nsub, slot_tok, h, y_init, w_gu, w_gu,
      b_gu.reshape(depth, n_e, two_f // ft2, ft2), w_dn, b_dn.reshape(depth, n_e, n_dn, MOE_DN_TILE))


def _moe_combine_kernel(pos_ref, x_ref, g_ref, gate_ref, y_hbm, o_ref, ybuf_ref, sem):
    i = pl.program_id(0)
    n = pl.num_programs(0)
    tt = x_ref.shape[0]

    def issue(step):
        sl = step % 2

        def body(u, carry):
            for q in range(DMA_UNROLL // TOP_K):
                t = u * (DMA_UNROLL // TOP_K) + q
                for k in range(TOP_K):
                    p = pos_ref[(step * tt + t) * TOP_K + k]
                    pltpu.make_async_copy(y_hbm.at[p], ybuf_ref.at[sl, k, t], sem.at[sl]).start()
            return carry

        lax.fori_loop(0, tt * TOP_K // DMA_UNROLL, body, 0)

    @pl.when(i == 0)
    def _():
        issue(i)

    @pl.when(i + 1 < n)
    def _():
        issue(i + 1)

    slot = i % 2
    pltpu.make_async_copy(ybuf_ref.at[slot], ybuf_ref.at[slot], sem.at[slot]).wait()
    gates = g_ref[...]
    tot = None
    for k in range(TOP_K):
        term = gates[:, k:k + 1] * ybuf_ref[slot, k]
        tot = term if tot is None else tot + term
    o_ref[...] = x_ref[...] + gate_ref[0] * tot


def _moe_combine(x, y, pos, gates, gate_mod):
    b, s, d = x.shape
    tt = COMBINE_TOKENS
    n_tok = b * s
    grid_spec = pltpu.PrefetchScalarGridSpec(
        num_scalar_prefetch=1,
        grid=(n_tok // tt,),
        in_specs=[
            pl.BlockSpec((tt, d), lambda i, pos_r: (i, 0)),
            pl.BlockSpec((tt, TOP_K), lambda i, pos_r: (i, 0)),
            pl.BlockSpec((1, 1, d), lambda i, pos_r: (i * tt // s, 0, 0)),
            pl.BlockSpec(memory_space=pl.ANY),
        ],
        out_specs=pl.BlockSpec((tt, d), lambda i, pos_r: (i, 0)),
        scratch_shapes=[pltpu.VMEM((2, TOP_K, tt, d), F32), pltpu.SemaphoreType.DMA((2,))],
    )
    out = pl.pallas_call(
        _moe_combine_kernel,
        grid_spec=grid_spec,
        out_shape=jax.ShapeDtypeStruct((n_tok, d), F32),
        compiler_params=_cparams(("arbitrary",)),
        name="moe_combine",
    )(pos, x.reshape(n_tok, d), gates.reshape(n_tok, TOP_K), gate_mod, y)
    return out.reshape(b, s, d)


def _moe_ffn(x, g, scale, shift, gate_mod, rw, rb, w_gu, b_gu, w_dn, b_dn, layer, y_buf):
    b, s, d = x.shape
    h, top_idx, gates = _norm_router(x, g, scale, shift, rw, rb)
    item_e, item_row0, item_nsub, slot_tok, pos = _moe_plan(top_idx.reshape(b * s, TOP_K), b * s)
    if y_buf is None:
        y_buf = jnp.zeros((slot_tok.shape[0], d), F32)
    y = _moe_experts(h.reshape(b * s, d), y_buf, (item_e, item_row0, item_nsub, slot_tok),
                     w_gu, b_gu, w_dn, b_dn, layer)
    return _moe_combine(x, y, pos, gates, gate_mod), y


def _rope_lane_tables(seq):
    inv_freq = ROPE_THETA ** (-jnp.arange(0, HEAD_DIM, 2, dtype=F32) / HEAD_DIM)
    ang = jnp.arange(seq, dtype=F32)[:, None] * inv_freq[None, :]
    cos, sin = jnp.cos(ang), jnp.sin(ang)
    reps = LANES // HEAD_DIM
    return (jnp.tile(jnp.concatenate([cos, cos], axis=-1), (1, reps)),
            jnp.tile(jnp.concatenate([-sin, sin], axis=-1), (1, reps)))


def _col_modes(kinds, tn, width):
    modes = []
    for n_cols, mode in kinds:
        assert n_cols % tn == 0
        modes += [mode] * (n_cols // tn)
    assert len(modes) * tn == width
    return jnp.asarray(modes, jnp.int32)


PLAIN, ROPE, SCALE, ROPE_SCALE = 0, 1, 2, 3


def kernel(x, c, ada_w, ada_b, norm_g, final_g, ab_w_in, ab_w_out, na_rpb, diff_lam_q1, diff_lam_k1,
           diff_lam_q2, diff_lam_k2, diff_subln_g, dil_w_in, dil_w_out, moe_router_w, moe_router_b,
           moe_w_gate_up, moe_b_gate_up, moe_w_down, moe_b_down):
    b, s, d = x.shape
    depth = ada_w.shape[0]
    tn = PROJ_TN
    cos_t, sin_t = _rope_lane_tables(s)
    y_buf = None

    c_pad = jnp.zeros((8, d), F32).at[:b].set(c)
    mod = _adaln(c_pad, ada_w.reshape(depth * 2, d, 3 * d), ada_b.reshape(depth * 2, 3 * d))
    mod = mod[:, :b].reshape(depth, 2, b, 3, 1, d)

    na_w = NA_HEADS * HEAD_DIM
    df_w = DIFF_HEADS * 2 * HEAD_DIM
    dil_w = DIL_HEADS * HEAD_DIM

    for l in range(depth):
        shift, scale, gate = (mod[l, 0, :, t] for t in range(3))
        h = _norm_mod(x, norm_g[l, 0], scale, shift, BF16)
        i = l // 2
        if l % 2 == 0:
            lambda_init = 0.8 - 0.6 * math.exp(-0.3 * l)
            modes = _col_modes([(na_w, SCALE), (2 * na_w, PLAIN), (df_w, ROPE_SCALE), (df_w, ROPE),
                                (df_w, PLAIN)], tn, ab_w_in.shape[2])
            proj = _proj(h, ab_w_in[i], modes, cos_t, sin_t, ab_w_in.shape[2], tn=tn)
            proj = proj.reshape(b, s, ab_w_in.shape[2])
            o_na = _na_attention(proj, _na_bias_table(na_rpb[i], s // GRID_W))
            o_df = _diff_attention(proj, diff_lam_q1[i], diff_lam_k1[i], diff_lam_q2[i], diff_lam_k2[i],
                                   diff_subln_g[i], lambda_init, col0=3 * na_w // LANES)
            o = jnp.concatenate([o_na, o_df], axis=-1)
            x = _outproj(o, ab_w_out[i], x, gate)
        else:
            modes = _col_modes([(dil_w, ROPE_SCALE), (dil_w, ROPE), (dil_w, PLAIN)] * len(DIL_CONFIGS),
                               tn, dil_w_in.shape[2])
            outs, lses = [], []
            for gi, (window, dil) in enumerate(DIL_CONFIGS):
                proj = _proj(h, dil_w_in[i], modes, cos_t, sin_t, 3 * dil_w, jbase=gi * (3 * dil_w // tn),
                             dil=dil, tn=tn)
                o_g, lse_g = _dil_attention(proj, window, dil)
                outs.append(o_g)
                lses.append(lse_g)
            o = _dil_mix(outs, lses, [dil for _, dil in DIL_CONFIGS])
            x = _outproj(o, dil_w_out[i], x, gate)
        shift, scale, gate = (mod[l, 1, :, t] for t in range(3))
        x, y_buf = _moe_ffn(x, norm_g[l, 1], scale, shift, gate, moe_router_w[l], moe_router_b[l],
                            moe_w_gate_up, moe_b_gate_up, moe_w_down, moe_b_down, l, y_buf)
    return _final_norm(x, final_g)
```
